```python
import math
import jax, jax.numpy as jnp
from jax import lax
import numpy as np

D_MODEL = 1024
BATCH = 4
SEQ = 8192
DEPTH = 1
DEC_BATCH = 16
DEC_SEQ = 4096
PAST_LEN = 128

HEAD_DIM = 64
A_HEADS = 8
A_WIDTH = A_HEADS * HEAD_DIM
A_GROUPS = ((128, 1), (512, 4), (2048, 16))
B_HEADS = 8
B_KV_HEADS = 2
B_GROUP = B_HEADS // B_KV_HEADS
B_WIDTH = B_HEADS * HEAD_DIM
B_KV_WIDTH = B_KV_HEADS * HEAD_DIM
B_HALF_WINDOW = 128
ROPE_THETA = 10000.0
LN_EPS = 1e-5
NEG_INF = -1e30
DEEPNORM_ALPHA = (2.0 * DEPTH) ** 0.25
DEEPNORM_BETA = (8.0 * DEPTH) ** -0.25
IN_SPLIT_SIZES = (A_WIDTH, A_WIDTH, A_WIDTH, A_WIDTH, B_WIDTH, B_KV_WIDTH, B_KV_WIDTH, B_WIDTH, D_MODEL, D_MODEL)
IN_WIDTH = sum(IN_SPLIT_SIZES)
IN_SPLIT_POINTS = tuple(int(v) for v in np.cumsum(IN_SPLIT_SIZES)[:-1])

kernel_name = "hybrid_dilated_window_encoder"


def layer_norm(z, g, b):
    zf = z.astype(jnp.float32)
    mu = zf.mean(-1, keepdims=True)
    var = jnp.mean(jnp.square(zf - mu), -1, keepdims=True)
    out = (zf - mu) * lax.rsqrt(var + LN_EPS) * g.astype(jnp.float32) + b.astype(jnp.float32)
    return out.astype(z.dtype)


def rope(x, pos):
    half = HEAD_DIM // 2
    inv_freq = ROPE_THETA ** (-jnp.arange(half, dtype=jnp.float32) / half)
    ang = pos.astype(jnp.float32)[:, None] * inv_freq[None, :]
    cos = jnp.cos(ang)[None, :, None, :]
    sin = jnp.sin(ang)[None, :, None, :]
    x1 = x[..., :half].astype(jnp.float32)
    x2 = x[..., half:].astype(jnp.float32)
    out = jnp.concatenate([x1 * cos - x2 * sin, x2 * cos + x1 * sin], axis=-1)
    return out.astype(x.dtype)


def banded_attention(q, k, v, half_window, sink=None):
    blk = half_window
    n, L = q.shape[0], q.shape[1]
    hkv, g, dh = q.shape[2], q.shape[3], q.shape[4]
    nb = -(-L // blk)
    lp = nb * blk
    pad = lp - L
    qb = jnp.pad(q, ((0, 0), (0, pad), (0, 0), (0, 0), (0, 0))).reshape(n, nb, blk, hkv, g, dh)

    def frames(t):
        t = jnp.pad(t, ((0, 0), (blk, pad + blk), (0, 0), (0, 0))).reshape(n, nb + 2, blk, hkv, dh)
        return jnp.concatenate([t[:, :-2], t[:, 1:-1], t[:, 2:]], axis=2)

    kb, vb = frames(k), frames(v)
    scale = 1.0 / math.sqrt(dh)
    s = jnp.einsum("nbqhgd,nbkhd->nbhgqk", qb, kb, preferred_element_type=jnp.float32) * scale
    qi = jnp.arange(nb)[:, None] * blk + jnp.arange(blk)[None, :]
    kj = (jnp.arange(nb)[:, None] - 1) * blk + jnp.arange(3 * blk)[None, :]
    valid = (jnp.abs(qi[:, :, None] - kj[:, None, :]) <= half_window) & (kj[:, None, :] >= 0) & (kj[:, None, :] < L)
    s = jnp.where(valid[None, :, None, None, :, :], s, NEG_INF)
    m = s.max(-1)
    if sink is not None:
        sink_f = sink.astype(jnp.float32)[None, None, :, :, None]
        m = jnp.maximum(m, sink_f)
    p = jnp.exp(s - m[..., None])
    denom = p.sum(-1)
    if sink is not None:
        denom = denom + jnp.exp(sink_f - m)
    o = jnp.einsum("nbhgqk,nbkhd->nbqhgd", p.astype(v.dtype), vb, preferred_element_type=jnp.float32)
    o = o / jnp.transpose(denom, (0, 1, 4, 2, 3))[..., None]
    lse = jnp.transpose(m + jnp.log(denom), (0, 1, 4, 2, 3))
    o = o.reshape(n, lp, hkv, g, dh)[:, :L]
    lse = lse.reshape(n, lp, hkv, g)[:, :L]
    return o, lse


def dilated_mixture_attention(q, k, v):
    b, s, h, dh = q.shape
    outs, lses = [], []
    for window, dil in A_GROUPS:
        L = s // dil

        def to_classes(t):
            return t.reshape(b, L, dil, h, dh).transpose(0, 2, 1, 3, 4).reshape(b * dil, L, h, dh)

        o, lse = banded_attention(to_classes(q)[:, :, :, None], to_classes(k), to_classes(v), window // (2 * dil))
        o = o[:, :, :, 0].reshape(b, dil, L, h, dh).transpose(0, 2, 1, 3, 4).reshape(b, s, h, dh)
        lse = lse[..., 0].reshape(b, dil, L, h).transpose(0, 2, 1, 3).reshape(b, s, h)
        outs.append(o)
        lses.append(lse)
    wts = jax.nn.softmax(jnp.stack(lses, 0), axis=0)
    return jnp.einsum("gbsh,gbshd->bshd", wts, jnp.stack(outs, 0))


def encoder_layer(x, w_in, b_gate, sink, w_br_a, w_br_b, w_out, ln_g, ln_b):
    b, s, _ = x.shape
    h = jnp.einsum("bsd,de->bse", x, w_in)
    qa, ka, va, ga, qb, kb, vb, gb, pre_a, pre_b = jnp.split(h, IN_SPLIT_POINTS, axis=-1)
    pos = jnp.arange(s)
    qa = rope(qa.reshape(b, s, A_HEADS, HEAD_DIM), pos)
    ka = rope(ka.reshape(b, s, A_HEADS, HEAD_DIM), pos)
    va = va.reshape(b, s, A_HEADS, HEAD_DIM)
    ya = dilated_mixture_attention(qa, ka, va).reshape(b, s, A_WIDTH).astype(x.dtype) * jax.nn.silu(ga)
    qb = rope(qb.reshape(b, s, B_HEADS, HEAD_DIM), pos).reshape(b, s, B_KV_HEADS, B_GROUP, HEAD_DIM)
    kb = rope(kb.reshape(b, s, B_KV_HEADS, HEAD_DIM), pos)
    vb = vb.reshape(b, s, B_KV_HEADS, HEAD_DIM)
    ob, _ = banded_attention(qb, kb, vb, B_HALF_WINDOW, sink=sink.reshape(B_KV_HEADS, B_GROUP))
    yb = ob.reshape(b, s, B_WIDTH).astype(x.dtype) * jax.nn.silu(gb)
    gate_a = jax.nn.sigmoid(pre_a + b_gate[0])
    gate_b = jax.nn.sigmoid(pre_b + b_gate[1])
    merged = gate_a * jnp.einsum("bse,ed->bsd", ya, w_br_a) + gate_b * jnp.einsum("bse,ed->bsd", yb, w_br_b)
    out = jnp.einsum("bsd,de->bse", merged, w_out)
    return layer_norm(DEEPNORM_ALPHA * x + out, ln_g, ln_b)


def setup_inputs(seed: int = 0) -> dict:
    key = jax.random.key(seed)
    ks = jax.random.split(key, 10)
    f32 = jnp.float32
    x_prompt = jax.random.normal(ks[0], (BATCH, SEQ, D_MODEL), f32)
    x_sample = jax.random.normal(ks[1], (DEC_BATCH, DEC_SEQ, D_MODEL), f32)
    w_in = jax.random.normal(ks[2], (DEPTH, D_MODEL, IN_WIDTH), f32) * D_MODEL ** -0.5
    b_gate = jax.random.normal(ks[3], (DEPTH, 2, D_MODEL), f32) * 0.02
    sink_logit = jax.random.normal(ks[4], (DEPTH, B_HEADS), f32) * 0.5
    w_branch_a = jax.random.normal(ks[5], (DEPTH, A_WIDTH, D_MODEL), f32) * (A_WIDTH ** -0.5 * DEEPNORM_BETA)
    w_branch_b = jax.random.normal(ks[6], (DEPTH, B_WIDTH, D_MODEL), f32) * (B_WIDTH ** -0.5 * DEEPNORM_BETA)
    w_out = jax.random.normal(ks[7], (DEPTH, D_MODEL, D_MODEL), f32) * (D_MODEL ** -0.5 * DEEPNORM_BETA)
    ln_gain = 1.0 + 0.02 * jax.random.normal(ks[8], (DEPTH, D_MODEL), f32)
    ln_bias = 0.02 * jax.random.normal(ks[9], (DEPTH, D_MODEL), f32)
    return {"x_prompt": x_prompt, "x_sample": x_sample, "w_in": w_in, "b_gate": b_gate,
            "sink_logit": sink_logit, "w_branch_a": w_branch_a, "w_branch_b": w_branch_b,
            "w_out": w_out, "ln_gain": ln_gain, "ln_bias": ln_bias}


def reference(x_prompt, x_sample, w_in, b_gate, sink_logit, w_branch_a, w_branch_b, w_out, ln_gain, ln_bias):
    y_prompt = x_prompt
    y_sample = x_sample
    for l in range(DEPTH):
        y_prompt = encoder_layer(y_prompt, w_in[l], b_gate[l], sink_logit[l], w_branch_a[l], w_branch_b[l],
                                 w_out[l], ln_gain[l], ln_bias[l])
        y_sample = encoder_layer(y_sample, w_in[l], b_gate[l], sink_logit[l], w_branch_a[l], w_branch_b[l],
                                 w_out[l], ln_gain[l], ln_bias[l])
    return (y_prompt, y_sample)
```

```python
import functools
import math

import numpy as np
import jax
import jax.numpy as jnp
from jax import lax
from jax.experimental import pallas as pl
from jax.experimental.pallas import tpu as pltpu

D_MODEL = 1024
HEAD_DIM = 64
HALF_DIM = HEAD_DIM // 2
A_HEADS = 8
A_WIDTH = A_HEADS * HEAD_DIM
A_DILATIONS = (1, 4, 16)
A_HALF_WINDOW = 64
B_HEADS = 8
B_KV_HEADS = 2
B_GROUP = B_HEADS // B_KV_HEADS
B_WIDTH = B_HEADS * HEAD_DIM
B_KV_WIDTH = B_KV_HEADS * HEAD_DIM
B_HALF_WINDOW = 128
ROPE_THETA = 10000.0
LN_EPS = 1e-5
NEG_INF = -1e30
DEPTH = 1
DEEPNORM_ALPHA = (2.0 * DEPTH) ** 0.25
IN_SPLIT_SIZES = (A_WIDTH, A_WIDTH, A_WIDTH, A_WIDTH, B_WIDTH, B_KV_WIDTH, B_KV_WIDTH, B_WIDTH, D_MODEL, D_MODEL)
IN_OFFSETS = tuple(int(v) for v in np.cumsum((0,) + IN_SPLIT_SIZES[:-1]))

LANES = 128
LOG2E = math.log2(math.e)
QK_SCALE = LOG2E / math.sqrt(HEAD_DIM)

A_TQ = 128
A_TK = A_TQ + 2 * A_HALF_WINDOW
A_SUPER = A_TQ * max(A_DILATIONS)
B_TQ = 128
B_TK = B_TQ + 2 * B_HALF_WINDOW
B_STEP = 1024

VMEM_LIMIT = 56 * 1024 * 1024

f32 = jnp.float32
bf16 = jnp.bfloat16


def _pair_tile_cols(head_of_slot):
    cols = []
    for part in range(4):
        head = head_of_slot[part % 2]
        base = head * HEAD_DIM + (part // 2) * HALF_DIM
        cols.extend(range(base, base + HALF_DIM))
    return cols


def _projection_columns():
    o = IN_OFFSETS
    cols = []
    for blk in (0, 1):
        for t in range(A_HEADS // 2):
            cols += [o[blk] + c for c in _pair_tile_cols((2 * t, 2 * t + 1))]
    cols += list(range(o[2], o[2] + A_WIDTH))
    cols += list(range(o[3], o[3] + A_WIDTH))
    for t in range(B_HEADS // 2):
        cols += [o[4] + c for c in _pair_tile_cols((2 * t, 2 * t + 1))]
    for g in range(B_KV_HEADS):
        cols += [o[5] + c for c in _pair_tile_cols((g, g))]
    for g in range(B_KV_HEADS):
        cols += [o[6] + g * HEAD_DIM + (l % HEAD_DIM) for l in range(LANES)]
    cols += list(range(o[7], o[7] + B_WIDTH))
    cols += list(range(o[8], o[8] + D_MODEL))
    cols += list(range(o[9], o[9] + D_MODEL))
    return np.asarray(cols, dtype=np.int32)


PROJ_COLS = _projection_columns()
P_QA, P_KA, P_VA, P_GA = 0, 512, 1024, 1536
P_QB, P_KB, P_VB, P_GB = 2048, 2560, 2816, 3072
P_PA, P_PB = 3584, 4608
P_TOTAL = 5632
assert PROJ_COLS.shape[0] == P_TOTAL


def _rope_tables(seq):
    inv_freq = ROPE_THETA ** (-jnp.arange(HALF_DIM, dtype=f32) / HALF_DIM)
    ang = jnp.arange(seq, dtype=f32)[:, None] * inv_freq[None, :]
    cos = jnp.tile(jnp.cos(ang), (1, 4))
    sin = jnp.tile(jnp.sin(ang), (1, 4))
    sign = jnp.where(jnp.arange(LANES) < 2 * HALF_DIM, -1.0, 1.0).astype(f32)
    sin = sin * sign[None, :]
    return cos * QK_SCALE, sin * QK_SCALE, cos, sin


def _band_bias(tq, tk, half_window):
    i = np.arange(tq)[:, None]
    j = np.arange(tk)[None, :]
    out = []
    for shift in (0, half_window, tk - tq):
        ok = np.abs(j - shift - i) <= half_window
        out.append(np.where(ok, 0.0, NEG_INF).astype(np.float32))
    b = np.stack(out, 0)
    return np.concatenate([b, b], axis=1)


def _proj_kernel(x_ref, w_ref, bg_ref, cq_ref, sq_ref, ck_ref, sk_ref,
                 qa_o, ka_o, va_o, ga_o, qb_o, kb_o, vb_o, gb_o, pa_o, pb_o):
    xb = x_ref[...].astype(bf16)

    def proj(c0, n):
        return jnp.dot(xb, w_ref[:, c0:c0 + n], preferred_element_type=f32)

    def rope(h, c, s):
        tiles = []
        for t in range(h.shape[1] // LANES):
            ht = h[:, t * LANES:(t + 1) * LANES]
            tiles.append(ht * c + pltpu.roll(ht, 2 * HALF_DIM, 1) * s)
        return jnp.concatenate(tiles, axis=1)

    cq, sq, ck, sk = cq_ref[...], sq_ref[...], ck_ref[...], sk_ref[...]
    qa_o[...] = rope(proj(P_QA, 512), cq, sq)
    ka_o[...] = rope(proj(P_KA, 512), ck, sk)
    va_o[...] = proj(P_VA, 512)
    ga_o[...] = jax.nn.silu(proj(P_GA, 512))
    qb_o[...] = rope(proj(P_QB, 512), cq, sq)
    kb_o[...] = rope(proj(P_KB, 256), ck, sk)
    vb_o[...] = proj(P_VB, 256)
    gb_o[...] = jax.nn.silu(proj(P_GB, 512))
    pa_o[...] = jax.nn.sigmoid(proj(P_PA, 1024) + bg_ref[0:1, :])
    pb_o[...] = jax.nn.sigmoid(proj(P_PB, 1024) + bg_ref[1:2, :])


def _project(x, w_p, b_gate, tables, tm=256):
    b, s, _ = x.shape
    nt = s // tm
    widths = (512, 512, 512, 512, 512, 256, 256, 512, 1024, 1024)
    tok = lambda n: pl.BlockSpec((None, tm, n), lambda bi, i: (bi, i, 0))
    tab = pl.BlockSpec((tm, LANES), lambda bi, i: (i, 0))
    return pl.pallas_call(
        _proj_kernel,
        grid=(b, nt),
        in_specs=[tok(D_MODEL),
                  pl.BlockSpec((D_MODEL, P_TOTAL), lambda bi, i: (0, 0)),
                  pl.BlockSpec((2, D_MODEL), lambda bi, i: (0, 0)),
                  tab, tab, tab, tab],
        out_specs=[tok(n) for n in widths],
        out_shape=[jax.ShapeDtypeStruct((b, s, n), f32) for n in widths],
        compiler_params=pltpu.CompilerParams(
            dimension_semantics=("parallel", "parallel"), vmem_limit_bytes=VMEM_LIMIT),
        name="in_proj",
    )(x, w_p, b_gate, *tables)


def _pair_attention(qf, kf, vf, bias2, sink_col=None):
    tq = qf.shape[0]
    lane = lax.broadcasted_iota(jnp.int32, (1, LANES), 1)
    slot0 = (lane % HEAD_DIM) < HALF_DIM
    qb = qf.astype(bf16)
    zero = jnp.zeros_like(qb)
    q2 = jnp.concatenate([jnp.where(slot0, qb, zero), jnp.where(slot0, zero, qb)], axis=0)
    s = lax.dot_general(q2, kf.astype(bf16), (((1,), (1,)), ((), ())), preferred_element_type=f32)
    s = s + bias2
    m = jnp.max(s, axis=-1, keepdims=True)
    if sink_col is not None:
        m = jnp.maximum(m, sink_col)
    p = jnp.exp2(s - m)
    l = jnp.sum(p, axis=-1, keepdims=True)
    if sink_col is not None:
        l = l + jnp.exp2(sink_col - m)
    acc = jnp.dot(p.astype(bf16), vf.astype(bf16), preferred_element_type=f32)
    first = lane < HEAD_DIM
    pick = lambda a: jnp.where(first, a[:tq], a[tq:])
    return pick(acc), pick(m), pick(l)


def _window(q0, length, tq, tk, half_window):
    w0 = jnp.clip(q0 - half_window, 0, length - tk)
    variant = jnp.where(q0 == 0, 0, jnp.where(q0 == length - tq, 2, 1))
    return w0, variant


def _mixer_a_kernel(q_ref, k_ref, v_ref, g_ref, bias_ref, o_ref, acc_s, m_s, l_s, *, seq):
    st = pl.program_id(2)
    for gi, dil in enumerate(A_DILATIONS):
        length = seq // dil
        rows_per_super = A_SUPER // dil
        nblk = rows_per_super // A_TQ

        def body(blk, carry, gi=gi, dil=dil, length=length, rows_per_super=rows_per_super):
            q0 = st * rows_per_super + blk * A_TQ
            w0, variant = _window(q0, length, A_TQ, A_TK, A_HALF_WINDOW)
            bias2 = bias_ref[variant]
            qbase = pl.multiple_of(blk * (A_TQ * dil), A_TQ)
            kbase = pl.multiple_of(w0 * dil, A_HALF_WINDOW)
            for r in range(dil):
                if dil == 1:
                    qrows = pl.ds(qbase, A_TQ)
                    krows = pl.ds(kbase, A_TK)
                else:
                    qrows = pl.ds(qbase + r, A_TQ, stride=dil)
                    krows = pl.ds(kbase + r, A_TK, stride=dil)
                acc, m, l = _pair_attention(q_ref[qrows, :], k_ref[krows, :], v_ref[krows, :], bias2)
                acc_s[gi, qrows, :] = acc
                m_s[gi, qrows, :] = m
                l_s[gi, qrows, :] = l
            return carry

        lax.fori_loop(0, nblk, body, 0)

    m_all = jnp.maximum(jnp.maximum(m_s[0], m_s[1]), m_s[2])
    num = jnp.zeros((A_SUPER, LANES), f32)
    den = jnp.zeros((A_SUPER, LANES), f32)
    for gi in range(len(A_DILATIONS)):
        w = jnp.exp2(m_s[gi] - m_all)
        num = num + w * acc_s[gi]
        den = den + w * l_s[gi]
    o_ref[...] = ((num / den) * g_ref[...]).astype(o_ref.dtype)


def _mixer_a(qa, ka, va, ga, bias):
    b, s, _ = qa.shape
    nst = s // A_SUPER
    ntile = A_WIDTH // LANES
    tile = pl.BlockSpec((None, A_SUPER, LANES), lambda bi, t, st: (bi, st, t))
    full = pl.BlockSpec((None, s, LANES), lambda bi, t, st: (bi, 0, t))
    return pl.pallas_call(
        functools.partial(_mixer_a_kernel, seq=s),
        grid=(b, ntile, nst),
        in_specs=[tile, full, full, tile,
                  pl.BlockSpec((3, 2 * A_TQ, A_TK), lambda bi, t, st: (0, 0, 0))],
        out_specs=tile,
        out_shape=jax.ShapeDtypeStruct((b, s, A_WIDTH), bf16),
        scratch_shapes=[pltpu.VMEM((len(A_DILATIONS), A_SUPER, LANES), f32)] * 3,
        compiler_params=pltpu.CompilerParams(
            dimension_semantics=("parallel", "parallel", "arbitrary"), vmem_limit_bytes=VMEM_LIMIT),
        name="mixer_a",
    )(qa, ka, va, ga, bias)


def _mixer_b_kernel(sink_ref, q_ref, k_ref, v_ref, g_ref, bias_ref, o_ref, *, seq):
    t = pl.program_id(1)
    st = pl.program_id(2)
    sink_col = jnp.concatenate([jnp.full((B_TQ, 1), sink_ref[2 * t], f32),
                                jnp.full((B_TQ, 1), sink_ref[2 * t + 1], f32)], axis=0)

    def body(blk, carry):
        q0 = st * B_STEP + blk * B_TQ
        w0, variant = _window(q0, seq, B_TQ, B_TK, B_HALF_WINDOW)
        qrows = pl.ds(pl.multiple_of(blk * B_TQ, B_TQ), B_TQ)
        krows = pl.ds(pl.multiple_of(w0, B_HALF_WINDOW), B_TK)
        acc, _, l = _pair_attention(q_ref[qrows, :], k_ref[krows, :], v_ref[krows, :], bias_ref[variant], sink_col)
        o_ref[qrows, :] = ((acc / l) * g_ref[qrows, :]).astype(o_ref.dtype)
        return carry

    lax.fori_loop(0, B_STEP // B_TQ, body, 0)


def _mixer_b(sink2, qb, kb, vb, gb, bias):
    b, s, _ = qb.shape
    nst = s // B_STEP
    ntile = B_WIDTH // LANES
    pairs_per_kv = B_GROUP // 2
    tile = pl.BlockSpec((None, B_STEP, LANES), lambda bi, t, st, sk: (bi, st, t))
    full = pl.BlockSpec((None, s, LANES), lambda bi, t, st, sk: (bi, 0, t // pairs_per_kv))
    return pl.pallas_call(
        functools.partial(_mixer_b_kernel, seq=s),
        grid_spec=pltpu.PrefetchScalarGridSpec(
            num_scalar_prefetch=1,
            grid=(b, ntile, nst),
            in_specs=[tile, full, full, tile,
                      pl.BlockSpec((3, 2 * B_TQ, B_TK), lambda bi, t, st, sk: (0, 0, 0))],
            out_specs=tile,
        ),
        out_shape=jax.ShapeDtypeStruct((b, s, B_WIDTH), bf16),
        compiler_params=pltpu.CompilerParams(
            dimension_semantics=("parallel", "parallel", "arbitrary"), vmem_limit_bytes=VMEM_LIMIT),
        name="mixer_b",
    )(sink2, qb, kb, vb, gb, bias)


def _out_kernel(ya_ref, yb_ref, pa_ref, pb_ref, x_ref, wa_ref, wb_ref, wo_ref, lg_ref, lb_ref, o_ref):
    bra = jnp.dot(ya_ref[...], wa_ref[...], preferred_element_type=f32)
    brb = jnp.dot(yb_ref[...], wb_ref[...], preferred_element_type=f32)
    merged = pa_ref[...] * bra + pb_ref[...] * brb
    out = jnp.dot(merged.astype(bf16), wo_ref[...], preferred_element_type=f32)
    z = DEEPNORM_ALPHA * x_ref[...] + out
    mu = jnp.mean(z, axis=-1, keepdims=True)
    zc = z - mu
    var = jnp.mean(zc * zc, axis=-1, keepdims=True)
    o_ref[...] = zc * lax.rsqrt(var + LN_EPS) * lg_ref[...] + lb_ref[...]


def _finish(ya, yb, pa, pb, x, wa, wb, wo, lg, lb, tm=512):
    b, s, _ = x.shape
    tok = lambda n: pl.BlockSpec((None, tm, n), lambda bi, i: (bi, i, 0))
    const = lambda r, c: pl.BlockSpec((r, c), lambda bi, i: (0, 0))
    return pl.pallas_call(
        _out_kernel,
        grid=(b, s // tm),
        in_specs=[tok(A_WIDTH), tok(B_WIDTH), tok(D_MODEL), tok(D_MODEL), tok(D_MODEL),
                  const(A_WIDTH, D_MODEL), const(B_WIDTH, D_MODEL), const(D_MODEL, D_MODEL),
                  const(1, D_MODEL), const(1, D_MODEL)],
        out_specs=tok(D_MODEL),
        out_shape=jax.ShapeDtypeStruct((b, s, D_MODEL), f32),
        compiler_params=pltpu.CompilerParams(
            dimension_semantics=("parallel", "parallel"), vmem_limit_bytes=VMEM_LIMIT),
        name="out_proj",
    )(ya, yb, pa, pb, x, wa, wb, wo, lg, lb)


def _encoder_layer(x, w_p, b_gate, sink2, wa, wb, wo, lg, lb, bias_a, bias_b):
    s = x.shape[1]
    assert s % A_SUPER == 0 and s // max(A_DILATIONS) >= A_TK and s % B_STEP == 0 and s >= B_TK
    tables = _rope_tables(s)
    qa, ka, va, ga, qb, kb, vb, gb, pa, pb = _project(x, w_p, b_gate, tables)
    ya = _mixer_a(qa, ka, va, ga, bias_a)
    yb = _mixer_b(sink2, qb, kb, vb, gb, bias_b)
    return _finish(ya, yb, pa, pb, x, wa, wb, wo, lg, lb)


def kernel(x_prompt, x_sample, w_in, b_gate, sink_logit, w_branch_a, w_branch_b, w_out, ln_gain, ln_bias):
    assert w_in.shape[0] == DEPTH
    bias_a = jnp.asarray(_band_bias(A_TQ, A_TK, A_HALF_WINDOW))
    bias_b = jnp.asarray(_band_bias(B_TQ, B_TK, B_HALF_WINDOW))
    y_prompt, y_sample = x_prompt, x_sample
    for l in range(DEPTH):
        w_p = jnp.take(w_in[l], jnp.asarray(PROJ_COLS), axis=1).astype(bf16)
        params = (w_p, b_gate[l], sink_logit[l] * LOG2E,
                  w_branch_a[l].astype(bf16), w_branch_b[l].astype(bf16), w_out[l].astype(bf16),
                  ln_gain[l][None, :], ln_bias[l][None, :], bias_a, bias_b)
        y_prompt = _encoder_layer(y_prompt, *params)
        y_sample = _encoder_layer(y_sample, *params)
    return (y_prompt, y_sample)
```

```python
import functools
import math

import numpy as np
import jax
import jax.numpy as jnp
from jax import lax
from jax.experimental import pallas as pl
from jax.experimental.pallas import tpu as pltpu

D_MODEL = 1024
HEAD_DIM = 64
HALF_DIM = HEAD_DIM // 2
A_HEADS = 8
A_WIDTH = A_HEADS * HEAD_DIM
A_DILATIONS = (1, 4, 16)
A_HALF_WINDOW = 64
B_HEADS = 8
B_KV_HEADS = 2
B_GROUP = B_HEADS // B_KV_HEADS
B_WIDTH = B_HEADS * HEAD_DIM
B_KV_WIDTH = B_KV_HEADS * HEAD_DIM
B_HALF_WINDOW = 128
ROPE_THETA = 10000.0
LN_EPS = 1e-5
NEG_INF = -1e30
DEPTH = 1
DEEPNORM_ALPHA = (2.0 * DEPTH) ** 0.25
IN_SPLIT_SIZES = (A_WIDTH, A_WIDTH, A_WIDTH, A_WIDTH, B_WIDTH, B_KV_WIDTH, B_KV_WIDTH, B_WIDTH, D_MODEL, D_MODEL)
IN_OFFSETS = tuple(int(v) for v in np.cumsum((0,) + IN_SPLIT_SIZES[:-1]))

LANES = 128
LOG2E = math.log2(math.e)
QK_SCALE = LOG2E / math.sqrt(HEAD_DIM)

A_TQ = 128
A_TK = A_TQ + 2 * A_HALF_WINDOW
A_SUPER = A_TQ * max(A_DILATIONS)
B_TQ = 128
B_TK = B_TQ + 2 * B_HALF_WINDOW
B_STEP = 1024
A_MID = A_DILATIONS[1]
assert A_DILATIONS == (1, A_MID, A_MID * A_MID)
A_BLOCKS_IN_FLIGHT = 4
B_BLOCKS_IN_FLIGHT = 4

VMEM_LIMIT = 56 * 1024 * 1024

f32 = jnp.float32
bf16 = jnp.bfloat16


def _pair_tile_cols(head_of_slot):
    cols = []
    for part in range(4):
        head = head_of_slot[part % 2]
        base = head * HEAD_DIM + (part // 2) * HALF_DIM
        cols.extend(range(base, base + HALF_DIM))
    return cols


def _projection_columns():
    o = IN_OFFSETS
    cols = []
    for blk in (0, 1):
        for t in range(A_HEADS // 2):
            cols += [o[blk] + c for c in _pair_tile_cols((2 * t, 2 * t + 1))]
    cols += list(range(o[2], o[2] + A_WIDTH))
    cols += list(range(o[3], o[3] + A_WIDTH))
    for t in range(B_HEADS // 2):
        cols += [o[4] + c for c in _pair_tile_cols((2 * t, 2 * t + 1))]
    for g in range(B_KV_HEADS):
        cols += [o[5] + c for c in _pair_tile_cols((g, g))]
    for g in range(B_KV_HEADS):
        cols += [o[6] + g * HEAD_DIM + (l % HEAD_DIM) for l in range(LANES)]
    cols += list(range(o[7], o[7] + B_WIDTH))
    cols += list(range(o[8], o[8] + D_MODEL))
    cols += list(range(o[9], o[9] + D_MODEL))
    return np.asarray(cols, dtype=np.int32)


PROJ_COLS = _projection_columns()
P_QA, P_KA, P_VA, P_GA = 0, 512, 1024, 1536
P_QB, P_KB, P_VB, P_GB = 2048, 2560, 2816, 3072
P_PA, P_PB = 3584, 4608
P_TOTAL = 5632
assert PROJ_COLS.shape[0] == P_TOTAL


def _rope_tables(seq):
    inv_freq = ROPE_THETA ** (-jnp.arange(HALF_DIM, dtype=f32) / HALF_DIM)
    ang = jnp.arange(seq, dtype=f32)[:, None] * inv_freq[None, :]
    cos = jnp.tile(jnp.cos(ang), (1, 4))
    sin = jnp.tile(jnp.sin(ang), (1, 4))
    sign = jnp.where(jnp.arange(LANES) < 2 * HALF_DIM, -1.0, 1.0).astype(f32)
    sin = sin * sign[None, :]
    return cos * QK_SCALE, sin * QK_SCALE, cos, sin


def _band_bias(tq, tk, half_window):
    i = np.arange(tq)[:, None]
    j = np.arange(tk)[None, :]
    out = []
    for shift in (0, half_window, tk - tq):
        ok = np.abs(j - shift - i) <= half_window
        out.append(np.where(ok, 0.0, NEG_INF).astype(np.float32))
    b = np.stack(out, 0)
    return np.concatenate([b, b], axis=1)


def _proj_kernel(x_ref, w_ref, bg_ref, cq_ref, sq_ref, ck_ref, sk_ref,
                 qa_o, ka_o, va_o, ga_o, qb_o, kb_o, vb_o, gb_o, pa_o, pb_o):
    xb = x_ref[...].astype(bf16)

    def proj(c0, n):
        return jnp.dot(xb, w_ref[:, c0:c0 + n], preferred_element_type=f32)

    def rope(h, c, s):
        tiles = []
        for t in range(h.shape[1] // LANES):
            ht = h[:, t * LANES:(t + 1) * LANES]
            tiles.append(ht * c + pltpu.roll(ht, 2 * HALF_DIM, 1) * s)
        return jnp.concatenate(tiles, axis=1)

    cq, sq, ck, sk = cq_ref[...], sq_ref[...], ck_ref[...], sk_ref[...]
    qa_o[...] = rope(proj(P_QA, 512), cq, sq)
    ka_o[...] = rope(proj(P_KA, 512), ck, sk)
    va_o[...] = proj(P_VA, 512)
    ga_o[...] = jax.nn.silu(proj(P_GA, 512))
    qb_o[...] = rope(proj(P_QB, 512), cq, sq)
    kb_o[...] = rope(proj(P_KB, 256), ck, sk)
    vb_o[...] = proj(P_VB, 256)
    gb_o[...] = jax.nn.silu(proj(P_GB, 512))
    pa_o[...] = jax.nn.sigmoid(proj(P_PA, 1024) + bg_ref[0:1, :])
    pb_o[...] = jax.nn.sigmoid(proj(P_PB, 1024) + bg_ref[1:2, :])


def _project(x, w_p, b_gate, tables, tm=256):
    b, s, _ = x.shape
    nt = s // tm
    widths = (512, 512, 512, 512, 512, 256, 256, 512, 1024, 1024)
    tok = lambda n: pl.BlockSpec((None, tm, n), lambda bi, i: (bi, i, 0))
    tab = pl.BlockSpec((tm, LANES), lambda bi, i: (i, 0))
    return pl.pallas_call(
        _proj_kernel,
        grid=(b, nt),
        in_specs=[tok(D_MODEL),
                  pl.BlockSpec((D_MODEL, P_TOTAL), lambda bi, i: (0, 0)),
                  pl.BlockSpec((2, D_MODEL), lambda bi, i: (0, 0)),
                  tab, tab, tab, tab],
        out_specs=[tok(n) for n in widths],
        out_shape=[jax.ShapeDtypeStruct((b, s, n), f32) for n in widths],
        compiler_params=pltpu.CompilerParams(
            dimension_semantics=("parallel", "parallel"), vmem_limit_bytes=VMEM_LIMIT),
        name="in_proj",
    )(x, w_p, b_gate, *tables)


def _pair_attention(qf, kf, vf, bias2, sink_b=None):
    tq, tk = qf.shape[0], kf.shape[0]
    lane = lax.broadcasted_iota(jnp.int32, (1, LANES), 1)
    slot0 = (lane % HEAD_DIM) < HALF_DIM
    qb = qf.astype(bf16)
    zero = jnp.zeros_like(qb)
    q2 = jnp.concatenate([jnp.where(slot0, qb, zero), jnp.where(slot0, zero, qb)], axis=0)
    s = lax.dot_general(q2, kf.astype(bf16), (((1,), (1,)), ((), ())), preferred_element_type=f32)
    s = s + bias2
    mb = jnp.broadcast_to(jnp.max(s, axis=-1, keepdims=True), (2 * tq, LANES))
    if sink_b is not None:
        mb = jnp.maximum(mb, sink_b)
    p = jnp.concatenate([jnp.exp2(s[:, c * LANES:(c + 1) * LANES] - mb) for c in range(tk // LANES)], axis=1)
    v2 = jnp.concatenate([vf.astype(bf16), jnp.ones((tk, LANES), bf16)], axis=1)
    acc2 = jnp.dot(p.astype(bf16), v2, preferred_element_type=f32)
    acc, l = acc2[:, :LANES], acc2[:, LANES:]
    if sink_b is not None:
        l = l + jnp.exp2(sink_b - mb)
    first = lane < HEAD_DIM
    pick = lambda a: jnp.where(first, a[:tq], a[tq:])
    return pick(acc), pick(mb), pick(l)


def _window(q0, length, tq, tk, half_window):
    w0 = jnp.clip(q0 - half_window, 0, length - tk)
    variant = jnp.where(q0 == 0, 0, jnp.where(q0 == length - tq, 2, 1))
    return w0, variant


def _mixer_a_kernel(q_ref, k_ref, v_ref, g_ref, bias_ref, o_ref, k4_s, v4_s, q4_s, acc_s, m_s, l_s, out_s, *, seq):
    st = pl.program_id(2)
    cls_len = seq // A_MID
    cls_rows = A_SUPER // A_MID

    @pl.when(st == 0)
    def _():
        def fill(c, carry):
            dst = pl.ds(pl.multiple_of(c * cls_rows, cls_rows), cls_rows)
            for r in range(A_MID):
                src = pl.ds(pl.multiple_of(c * A_SUPER, A_SUPER) + r, cls_rows, stride=A_MID)
                k4_s[r, dst, :] = k_ref[src, :]
                v4_s[r, dst, :] = v_ref[src, :]
            return carry
        lax.fori_loop(0, seq // A_SUPER, fill, 0)

    for r in range(A_MID):
        q4_s[r] = q_ref[pl.ds(r, cls_rows, stride=A_MID), :]

    def run(blocks_per_iter, n_iter, block):
        def body(it, carry):
            for u in range(blocks_per_iter):
                block(it * blocks_per_iter + u)
            return carry
        lax.fori_loop(0, n_iter, body, 0)

    def emit(gi, rows, res):
        acc_s[gi, rows, :], m_s[gi, rows, :], l_s[gi, rows, :] = res

    def block1(blk):
        q0 = st * A_SUPER + blk * A_TQ
        w0, variant = _window(q0, seq, A_TQ, A_TK, A_HALF_WINDOW)
        qrows = pl.ds(pl.multiple_of(blk * A_TQ, A_TQ), A_TQ)
        krows = pl.ds(pl.multiple_of(w0, A_HALF_WINDOW), A_TK)
        emit(0, qrows, _pair_attention(q_ref[qrows, :], k_ref[krows, :], v_ref[krows, :], bias_ref[variant]))

    run(A_BLOCKS_IN_FLIGHT, A_SUPER // A_TQ // A_BLOCKS_IN_FLIGHT, block1)

    def block4(blk):
        q0 = st * cls_rows + blk * A_TQ
        w0, variant = _window(q0, cls_len, A_TQ, A_TK, A_HALF_WINDOW)
        qrows = pl.ds(pl.multiple_of(blk * A_TQ, A_TQ), A_TQ)
        krows = pl.ds(pl.multiple_of(w0, A_HALF_WINDOW), A_TK)
        bias2 = bias_ref[variant]
        for r in range(A_MID):
            res = _pair_attention(q4_s[r, qrows, :], k4_s[r, krows, :], v4_s[r, krows, :], bias2)
            emit(1, pl.ds(pl.multiple_of(r * cls_rows + blk * A_TQ, A_TQ), A_TQ), res)

    run(1, cls_rows // A_TQ, block4)

    far_len = seq // A_DILATIONS[2]
    w0, variant = _window(st * A_TQ, far_len, A_TQ, A_TK, A_HALF_WINDOW)
    kbase = pl.multiple_of(w0 * A_MID, A_HALF_WINDOW)
    bias2 = bias_ref[variant]
    def block16(a):
        krows = pl.ds(kbase + a, A_TK, stride=A_MID)
        for r in range(A_MID):
            res = _pair_attention(q4_s[r, pl.ds(a, A_TQ, stride=A_MID), :],
                                  k4_s[r, krows, :], v4_s[r, krows, :], bias2)
            emit(2, pl.ds(r * cls_rows + a, A_TQ, stride=A_MID), res)

    run(1, A_MID, block16)

    for r in range(A_MID):
        tok = pl.ds(r, cls_rows, stride=A_MID)
        cls = pl.ds(r * cls_rows, cls_rows)
        ms = (m_s[0, tok, :], m_s[1, cls, :], m_s[2, cls, :])
        m_all = jnp.maximum(jnp.maximum(ms[0], ms[1]), ms[2])
        ws = [jnp.exp2(m - m_all) for m in ms]
        num = ws[0] * acc_s[0, tok, :] + ws[1] * acc_s[1, cls, :] + ws[2] * acc_s[2, cls, :]
        den = ws[0] * l_s[0, tok, :] + ws[1] * l_s[1, cls, :] + ws[2] * l_s[2, cls, :]
        out_s[tok, :] = (num / den) * g_ref[tok, :]
    o_ref[...] = out_s[...].astype(o_ref.dtype)


def _mixer_a(qa, ka, va, ga, bias):
    b, s, _ = qa.shape
    nst = s // A_SUPER
    ntile = A_WIDTH // LANES
    tile = pl.BlockSpec((None, A_SUPER, LANES), lambda bi, t, st: (bi, st, t))
    full = pl.BlockSpec((None, s, LANES), lambda bi, t, st: (bi, 0, t))
    groups = pltpu.VMEM((len(A_DILATIONS), A_SUPER, LANES), f32)
    return pl.pallas_call(
        functools.partial(_mixer_a_kernel, seq=s),
        grid=(b, ntile, nst),
        in_specs=[tile, full, full, tile,
                  pl.BlockSpec((3, 2 * A_TQ, A_TK), lambda bi, t, st: (0, 0, 0))],
        out_specs=tile,
        out_shape=jax.ShapeDtypeStruct((b, s, A_WIDTH), bf16),
        scratch_shapes=[pltpu.VMEM((A_MID, s // A_MID, LANES), f32),
                        pltpu.VMEM((A_MID, s // A_MID, LANES), f32),
                        pltpu.VMEM((A_MID, A_SUPER // A_MID, LANES), f32),
                        groups, groups, groups,
                        pltpu.VMEM((A_SUPER, LANES), f32)],
        compiler_params=pltpu.CompilerParams(
            dimension_semantics=("arbitrary", "arbitrary", "arbitrary"), vmem_limit_bytes=VMEM_LIMIT),
        name="mixer_a",
    )(qa, ka, va, ga, bias)


def _mixer_b_kernel(sink_ref, q_ref, k_ref, v_ref, g_ref, bias_ref, o_ref, *, seq):
    t = pl.program_id(1)
    st = pl.program_id(2)
    sink_b = jnp.concatenate([jnp.full((B_TQ, LANES), sink_ref[2 * t], f32),
                              jnp.full((B_TQ, LANES), sink_ref[2 * t + 1], f32)], axis=0)

    def block(blk):
        q0 = st * B_STEP + blk * B_TQ
        w0, variant = _window(q0, seq, B_TQ, B_TK, B_HALF_WINDOW)
        qrows = pl.ds(pl.multiple_of(blk * B_TQ, B_TQ), B_TQ)
        krows = pl.ds(pl.multiple_of(w0, B_HALF_WINDOW), B_TK)
        acc, _, l = _pair_attention(q_ref[qrows, :], k_ref[krows, :], v_ref[krows, :], bias_ref[variant], sink_b)
        o_ref[qrows, :] = ((acc / l) * g_ref[qrows, :]).astype(o_ref.dtype)

    def body(it, carry):
        for u in range(B_BLOCKS_IN_FLIGHT):
            block(it * B_BLOCKS_IN_FLIGHT + u)
        return carry

    lax.fori_loop(0, B_STEP // B_TQ // B_BLOCKS_IN_FLIGHT, body, 0)


def _mixer_b(sink2, qb, kb, vb, gb, bias):
    b, s, _ = qb.shape
    nst = s // B_STEP
    ntile = B_WIDTH // LANES
    pairs_per_kv = B_GROUP // 2
    tile = pl.BlockSpec((None, B_STEP, LANES), lambda bi, t, st, sk: (bi, st, t))
    full = pl.BlockSpec((None, s, LANES), lambda bi, t, st, sk: (bi, 0, t // pairs_per_kv))
    return pl.pallas_call(
        functools.partial(_mixer_b_kernel, seq=s),
        grid_spec=pltpu.PrefetchScalarGridSpec(
            num_scalar_prefetch=1,
            grid=(b, ntile, nst),
            in_specs=[tile, full, full, tile,
                      pl.BlockSpec((3, 2 * B_TQ, B_TK), lambda bi, t, st, sk: (0, 0, 0))],
            out_specs=tile,
        ),
        out_shape=jax.ShapeDtypeStruct((b, s, B_WIDTH), bf16),
        compiler_params=pltpu.CompilerParams(
            dimension_semantics=("parallel", "parallel", "arbitrary"), vmem_limit_bytes=VMEM_LIMIT),
        name="mixer_b",
    )(sink2, qb, kb, vb, gb, bias)


def _out_kernel(ya_ref, yb_ref, pa_ref, pb_ref, x_ref, wa_ref, wb_ref, wo_ref, lg_ref, lb_ref, o_ref):
    bra = jnp.dot(ya_ref[...], wa_ref[...], preferred_element_type=f32)
    brb = jnp.dot(yb_ref[...], wb_ref[...], preferred_element_type=f32)
    merged = pa_ref[...] * bra + pb_ref[...] * brb
    out = jnp.dot(merged.astype(bf16), wo_ref[...], preferred_element_type=f32)
    z = DEEPNORM_ALPHA * x_ref[...] + out
    mu = jnp.mean(z, axis=-1, keepdims=True)
    zc = z - mu
    var = jnp.mean(zc * zc, axis=-1, keepdims=True)
    o_ref[...] = zc * lax.rsqrt(var + LN_EPS) * lg_ref[...] + lb_ref[...]


def _finish(ya, yb, pa, pb, x, wa, wb, wo, lg, lb, tm=512):
    b, s, _ = x.shape
    tok = lambda n: pl.BlockSpec((None, tm, n), lambda bi, i: (bi, i, 0))
    const = lambda r, c: pl.BlockSpec((r, c), lambda bi, i: (0, 0))
    return pl.pallas_call(
        _out_kernel,
        grid=(b, s // tm),
        in_specs=[tok(A_WIDTH), tok(B_WIDTH), tok(D_MODEL), tok(D_MODEL), tok(D_MODEL),
                  const(A_WIDTH, D_MODEL), const(B_WIDTH, D_MODEL), const(D_MODEL, D_MODEL),
                  const(1, D_MODEL), const(1, D_MODEL)],
        out_specs=tok(D_MODEL),
        out_shape=jax.ShapeDtypeStruct((b, s, D_MODEL), f32),
        compiler_params=pltpu.CompilerParams(
            dimension_semantics=("parallel", "parallel"), vmem_limit_bytes=VMEM_LIMIT),
        name="out_proj",
    )(ya, yb, pa, pb, x, wa, wb, wo, lg, lb)


def _encoder_layer(x, w_p, b_gate, sink2, wa, wb, wo, lg, lb, bias_a, bias_b):
    s = x.shape[1]
    assert s % A_SUPER == 0 and s // max(A_DILATIONS) >= A_TK and s % B_STEP == 0 and s >= B_TK
    tables = _rope_tables(s)
    qa, ka, va, ga, qb, kb, vb, gb, pa, pb = _project(x, w_p, b_gate, tables)
    ya = _mixer_a(qa, ka, va, ga, bias_a)
    yb = _mixer_b(sink2, qb, kb, vb, gb, bias_b)
    return _finish(ya, yb, pa, pb, x, wa, wb, wo, lg, lb)


def kernel(x_prompt, x_sample, w_in, b_gate, sink_logit, w_branch_a, w_branch_b, w_out, ln_gain, ln_bias):
    assert w_in.shape[0] == DEPTH
    bias_a = jnp.asarray(_band_bias(A_TQ, A_TK, A_HALF_WINDOW))
    bias_b = jnp.asarray(_band_bias(B_TQ, B_TK, B_HALF_WINDOW))
    y_prompt, y_sample = x_prompt, x_sample
    for l in range(DEPTH):
        w_p = jnp.take(w_in[l], jnp.asarray(PROJ_COLS), axis=1).astype(bf16)
        params = (w_p, b_gate[l], sink_logit[l] * LOG2E,
                  w_branch_a[l].astype(bf16), w_branch_b[l].astype(bf16), w_out[l].astype(bf16),
                  ln_gain[l][None, :], ln_bias[l][None, :], bias_a, bias_b)
        y_prompt = _encoder_layer(y_prompt, *params)
        y_sample = _encoder_layer(y_sample, *params)
    return (y_prompt, y_sample)
```

```python
import functools
import math

import numpy as np
import jax
import jax.numpy as jnp
from jax import lax
from jax.experimental import pallas as pl
from jax.experimental.pallas import tpu as pltpu

D_MODEL = 1024
HEAD_DIM = 64
HALF_DIM = HEAD_DIM // 2
A_HEADS = 8
A_WIDTH = A_HEADS * HEAD_DIM
A_DILATIONS = (1, 4, 16)
A_HALF_WINDOW = 64
B_HEADS = 8
B_KV_HEADS = 2
B_GROUP = B_HEADS // B_KV_HEADS
B_WIDTH = B_HEADS * HEAD_DIM
B_KV_WIDTH = B_KV_HEADS * HEAD_DIM
B_HALF_WINDOW = 128
ROPE_THETA = 10000.0
LN_EPS = 1e-5
NEG_INF = -1e30
DEPTH = 1
DEEPNORM_ALPHA = (2.0 * DEPTH) ** 0.25
IN_SPLIT_SIZES = (A_WIDTH, A_WIDTH, A_WIDTH, A_WIDTH, B_WIDTH, B_KV_WIDTH, B_KV_WIDTH, B_WIDTH, D_MODEL, D_MODEL)
IN_OFFSETS = tuple(int(v) for v in np.cumsum((0,) + IN_SPLIT_SIZES[:-1]))

LANES = 128
LOG2E = math.log2(math.e)
QK_SCALE = LOG2E / math.sqrt(HEAD_DIM)

A_TQ = 128
A_TK = A_TQ + 2 * A_HALF_WINDOW
A_SUPER = A_TQ * max(A_DILATIONS)
B_TQ = 128
B_TK = B_TQ + 2 * B_HALF_WINDOW
B_STEP = 2048
A_MID = A_DILATIONS[1]
assert A_DILATIONS == (1, A_MID, A_MID * A_MID)
OUT_SUBTILES = 2
B_BLOCKS_PER_GROUP = 4

VMEM_LIMIT = 56 * 1024 * 1024

f32 = jnp.float32
bf16 = jnp.bfloat16


def _pair_tile_cols(head_of_slot):
    cols = []
    for part in range(4):
        head = head_of_slot[part % 2]
        base = head * HEAD_DIM + (part // 2) * HALF_DIM
        cols.extend(range(base, base + HALF_DIM))
    return cols


def _projection_columns():
    o = IN_OFFSETS
    cols = []
    for blk in (0, 1):
        for t in range(A_HEADS // 2):
            cols += [o[blk] + c for c in _pair_tile_cols((2 * t, 2 * t + 1))]
    cols += list(range(o[2], o[2] + A_WIDTH))
    cols += list(range(o[3], o[3] + A_WIDTH))
    for t in range(B_HEADS // 2):
        cols += [o[4] + c for c in _pair_tile_cols((2 * t, 2 * t + 1))]
    for g in range(B_KV_HEADS):
        cols += [o[5] + c for c in _pair_tile_cols((g, g))]
    for g in range(B_KV_HEADS):
        cols += [o[6] + g * HEAD_DIM + (l % HEAD_DIM) for l in range(LANES)]
    cols += list(range(o[7], o[7] + B_WIDTH))
    cols += list(range(o[8], o[8] + D_MODEL))
    cols += list(range(o[9], o[9] + D_MODEL))
    return np.asarray(cols, dtype=np.int32)


PROJ_COLS = _projection_columns()
P_QA, P_KA, P_VA, P_GA = 0, 512, 1024, 1536
P_QB, P_KB, P_VB, P_GB = 2048, 2560, 2816, 3072
P_PA, P_PB = 3584, 4608
P_TOTAL = 5632
assert PROJ_COLS.shape[0] == P_TOTAL


def _rope_tables(seq):
    inv_freq = ROPE_THETA ** (-jnp.arange(HALF_DIM, dtype=f32) / HALF_DIM)
    ang = jnp.arange(seq, dtype=f32)[:, None] * inv_freq[None, :]
    cos = jnp.tile(jnp.cos(ang), (1, 4))
    sin = jnp.tile(jnp.sin(ang), (1, 4))
    sign = jnp.where(jnp.arange(LANES) < 2 * HALF_DIM, -1.0, 1.0).astype(f32)
    sin = sin * sign[None, :]
    return cos * QK_SCALE, sin * QK_SCALE, cos, sin


def _band_bias(tq, tk, half_window):
    i = np.arange(tq)[:, None]
    j = np.arange(tk)[None, :]
    out = []
    for shift in (0, half_window, tk - tq):
        ok = np.abs(j - shift - i) <= half_window
        out.append(np.where(ok, 0.0, NEG_INF).astype(np.float32))
    b = np.stack(out, 0)
    return np.concatenate([b, b], axis=1)


def _proj_kernel(x_ref, w_ref, bg_ref, cq_ref, sq_ref, ck_ref, sk_ref,
                 qa_o, ka_o, va_o, ga_o, qb_o, kb_o, vb_o, gb_o, pa_o, pb_o):
    xb = x_ref[...].astype(bf16)

    def proj(c0, n):
        return jnp.dot(xb, w_ref[:, c0:c0 + n], preferred_element_type=f32)

    def rope(h, c, s):
        tiles = []
        for t in range(h.shape[1] // LANES):
            ht = h[:, t * LANES:(t + 1) * LANES]
            tiles.append(ht * c + pltpu.roll(ht, 2 * HALF_DIM, 1) * s)
        return jnp.concatenate(tiles, axis=1)

    cq, sq, ck, sk = cq_ref[...], sq_ref[...], ck_ref[...], sk_ref[...]
    qa_o[...] = rope(proj(P_QA, 512), cq, sq)
    ka_o[...] = rope(proj(P_KA, 512), ck, sk)
    va_o[...] = proj(P_VA, 512)
    ga_o[...] = jax.nn.silu(proj(P_GA, 512))
    qb_o[...] = rope(proj(P_QB, 512), cq, sq)
    kb_o[...] = rope(proj(P_KB, 256), ck, sk)
    vb_o[...] = proj(P_VB, 256)
    gb_o[...] = jax.nn.silu(proj(P_GB, 512))
    pa_o[...] = jax.nn.sigmoid(proj(P_PA, 1024) + bg_ref[0:1, :]).astype(pa_o.dtype)
    pb_o[...] = jax.nn.sigmoid(proj(P_PB, 1024) + bg_ref[1:2, :]).astype(pb_o.dtype)


def _project(x, w_p, b_gate, tables, tm=256):
    b, s, _ = x.shape
    nt = s // tm
    widths = (512, 512, 512, 512, 512, 256, 256, 512, 1024, 1024)
    dtypes = (f32,) * 8 + (bf16, bf16)
    tok = lambda n: pl.BlockSpec((None, tm, n), lambda bi, i: (bi, i, 0))
    tab = pl.BlockSpec((tm, LANES), lambda bi, i: (i, 0))
    return pl.pallas_call(
        _proj_kernel,
        grid=(b, nt),
        in_specs=[tok(D_MODEL),
                  pl.BlockSpec((D_MODEL, P_TOTAL), lambda bi, i: (0, 0)),
                  pl.BlockSpec((2, D_MODEL), lambda bi, i: (0, 0)),
                  tab, tab, tab, tab],
        out_specs=[tok(n) for n in widths],
        out_shape=[jax.ShapeDtypeStruct((b, s, n), dt) for n, dt in zip(widths, dtypes)],
        compiler_params=pltpu.CompilerParams(
            dimension_semantics=("parallel", "parallel"), vmem_limit_bytes=VMEM_LIMIT),
        name="in_proj",
    )(x, w_p, b_gate, *tables)


def _scores(qf, kf, bias2, sink_b=None):
    tq, tk = qf.shape[0], kf.shape[0]
    lane = lax.broadcasted_iota(jnp.int32, (1, LANES), 1)
    slot0 = (lane % HEAD_DIM) < HALF_DIM
    qb = qf.astype(bf16)
    zero = jnp.zeros_like(qb)
    q2 = jnp.concatenate([jnp.where(slot0, qb, zero), jnp.where(slot0, zero, qb)], axis=0)
    s = lax.dot_general(q2, kf.astype(bf16), (((1,), (1,)), ((), ())), preferred_element_type=f32)
    s = s + bias2
    mb = jnp.broadcast_to(jnp.max(s, axis=-1, keepdims=True), (2 * tq, LANES))
    if sink_b is not None:
        mb = jnp.maximum(mb, sink_b)
    p = jnp.concatenate([jnp.exp2(s[:, c * LANES:(c + 1) * LANES] - mb) for c in range(tk // LANES)], axis=1)
    return p.astype(bf16), mb


def _values(p, mb, vf):
    tq = p.shape[0] // 2
    first = lax.broadcasted_iota(jnp.int32, (1, LANES), 1) < HEAD_DIM
    vb = vf.astype(bf16)
    ones = jnp.ones_like(vb)
    o0 = jnp.dot(p[:tq], jnp.where(first, vb, ones), preferred_element_type=f32)
    o1 = jnp.dot(p[tq:], jnp.where(first, ones, vb), preferred_element_type=f32)
    return jnp.where(first, o0, o1), jnp.where(first, mb[:tq], mb[tq:]), jnp.where(first, o1, o0)


def _aligned(x, multiple):
    if isinstance(x, int):
        assert x % multiple == 0
        return x
    return pl.multiple_of(x, multiple)


def _window(q0, length, tq, tk, half_window):
    w0 = jnp.clip(q0 - half_window, 0, length - tk)
    variant = jnp.where(q0 == 0, 0, jnp.where(q0 == length - tq, 2, 1))
    return w0, variant


def _software_pipeline(groups, nb, stage1, stage2):
    for u in range(nb):
        stage1(groups[0], 0, u)
    for i in range(1, len(groups)):
        for u in range(nb):
            stage1(groups[i], i % 2, u)
        for u in range(nb):
            stage2(groups[i - 1], (i - 1) % 2, u)
    for u in range(nb):
        stage2(groups[-1], (len(groups) - 1) % 2, u)


def _mixer_a_kernel(q_ref, k_ref, v_ref, g_ref, bias_ref, o_ref,
                    k4_s, v4_s, q4_s, p_s, mb_s, acc_s, m_s, l_s, out_s, *, seq):
    st = pl.program_id(2)
    cls_len = seq // A_MID
    cls_rows = A_SUPER // A_MID

    @pl.when(st == 0)
    def _():
        def fill(c, carry):
            dst = pl.ds(_aligned(c * cls_rows, cls_rows), cls_rows)
            for r in range(A_MID):
                src = pl.ds(_aligned(c * A_SUPER, A_SUPER) + r, cls_rows, stride=A_MID)
                k4_s[r, dst, :] = k_ref[src, :]
                v4_s[r, dst, :] = v_ref[src, :]
            return carry
        lax.fori_loop(0, seq // A_SUPER, fill, 0)

    for r in range(A_MID):
        q4_s[r] = q_ref[pl.ds(r, cls_rows, stride=A_MID), :]

    nb = A_MID
    far_w0, far_variant = _window(st * A_TQ, seq // A_DILATIONS[2], A_TQ, A_TK, A_HALF_WINDOW)
    far_kbase = _aligned(far_w0 * A_MID, A_HALF_WINDOW)

    def geometry(kind, g, u):
        if kind == 0:
            blk = g * nb + u
            w0, variant = _window(st * A_SUPER + blk * A_TQ, seq, A_TQ, A_TK, A_HALF_WINDOW)
            qrows = pl.ds(_aligned(blk * A_TQ, A_TQ), A_TQ)
            krows = pl.ds(_aligned(w0, A_HALF_WINDOW), A_TK)
            return (lambda: q_ref[qrows, :]), krows, k_ref, v_ref, variant, qrows
        if kind == 1:
            w0, variant = _window(st * cls_rows + g * A_TQ, cls_len, A_TQ, A_TK, A_HALF_WINDOW)
            qrows = pl.ds(_aligned(g * A_TQ, A_TQ), A_TQ)
            krows = pl.ds(_aligned(w0, A_HALF_WINDOW), A_TK)
            out_rows = pl.ds(_aligned(u * cls_rows + g * A_TQ, A_TQ), A_TQ)
            return (lambda: q4_s[u, qrows, :]), krows, k4_s.at[u], v4_s.at[u], variant, out_rows
        qrows = pl.ds(g, A_TQ, stride=A_MID)
        krows = pl.ds(far_kbase + g, A_TK, stride=A_MID)
        out_rows = pl.ds(u * cls_rows + g, A_TQ, stride=A_MID)
        return (lambda: q4_s[u, qrows, :]), krows, k4_s.at[u], v4_s.at[u], far_variant, out_rows

    def stage1(group, slot, u):
        kind, g = group
        load_q, krows, kk, _, variant, _ = geometry(kind, g, u)
        p_s[slot, u], mb_s[slot, u] = _scores(load_q(), kk[krows, :], bias_ref[variant])

    def stage2(group, slot, u):
        kind, g = group
        _, krows, _, vv, _, out_rows = geometry(kind, g, u)
        acc_s[kind, out_rows, :], m_s[kind, out_rows, :], l_s[kind, out_rows, :] = _values(
            p_s[slot, u], mb_s[slot, u], vv[krows, :])

    groups = [(kind, g) for kind in range(len(A_DILATIONS)) for g in range(A_SUPER // A_TQ // nb)]
    _software_pipeline(groups, nb, stage1, stage2)

    for r in range(A_MID):
        tok = pl.ds(r, cls_rows, stride=A_MID)
        cls = pl.ds(r * cls_rows, cls_rows)
        rows = (tok, cls, cls)
        ms = [m_s[gi, rows[gi], :] for gi in range(3)]
        m_all = jnp.maximum(jnp.maximum(ms[0], ms[1]), ms[2])
        num = jnp.zeros((cls_rows, LANES), f32)
        den = jnp.zeros((cls_rows, LANES), f32)
        for gi in range(3):
            w = jnp.exp2(ms[gi] - m_all)
            num = num + w * acc_s[gi, rows[gi], :]
            den = den + w * pltpu.roll(l_s[gi, rows[gi], :], HEAD_DIM, 1)
        out_s[tok, :] = (num / den) * g_ref[tok, :]
    o_ref[...] = out_s[...].astype(o_ref.dtype)


def _mixer_a(qa, ka, va, ga, bias):
    b, s, _ = qa.shape
    nst = s // A_SUPER
    ntile = A_WIDTH // LANES
    tile = pl.BlockSpec((None, A_SUPER, LANES), lambda bi, t, st: (bi, st, t))
    full = pl.BlockSpec((None, s, LANES), lambda bi, t, st: (bi, 0, t))
    groups = pltpu.VMEM((len(A_DILATIONS), A_SUPER, LANES), f32)
    return pl.pallas_call(
        functools.partial(_mixer_a_kernel, seq=s),
        grid=(b, ntile, nst),
        in_specs=[tile, full, full, tile,
                  pl.BlockSpec((3, 2 * A_TQ, A_TK), lambda bi, t, st: (0, 0, 0))],
        out_specs=tile,
        out_shape=jax.ShapeDtypeStruct((b, s, A_WIDTH), bf16),
        scratch_shapes=[pltpu.VMEM((A_MID, s // A_MID, LANES), f32),
                        pltpu.VMEM((A_MID, s // A_MID, LANES), f32),
                        pltpu.VMEM((A_MID, A_SUPER // A_MID, LANES), f32),
                        pltpu.VMEM((2, A_MID, 2 * A_TQ, A_TK), bf16),
                        pltpu.VMEM((2, A_MID, 2 * A_TQ, LANES), f32),
                        groups, groups, groups,
                        pltpu.VMEM((A_SUPER, LANES), f32)],
        compiler_params=pltpu.CompilerParams(
            dimension_semantics=("arbitrary", "arbitrary", "arbitrary"), vmem_limit_bytes=VMEM_LIMIT),
        name="mixer_a",
    )(qa, ka, va, ga, bias)


def _mixer_b_kernel(sink_ref, q_ref, k_ref, v_ref, g_ref, bias_ref, o_ref, p_s, mb_s, *, seq):
    t = pl.program_id(1)
    st = pl.program_id(2)
    nb = B_BLOCKS_PER_GROUP
    sink0, sink1 = sink_ref[2 * t], sink_ref[2 * t + 1]
    sink_b = jnp.concatenate([jnp.full((B_TQ, LANES), sink0, f32), jnp.full((B_TQ, LANES), sink1, f32)], axis=0)
    first = lax.broadcasted_iota(jnp.int32, (B_TQ, LANES), 1) < HEAD_DIM
    sink_pair = jnp.where(first, sink0, sink1)

    def geometry(g, u):
        blk = g * nb + u
        w0, variant = _window(st * B_STEP + blk * B_TQ, seq, B_TQ, B_TK, B_HALF_WINDOW)
        qrows = pl.ds(_aligned(blk * B_TQ, B_TQ), B_TQ)
        krows = pl.ds(_aligned(w0, B_HALF_WINDOW), B_TK)
        return qrows, krows, variant

    def stage1(g, slot, u):
        qrows, krows, variant = geometry(g, u)
        p_s[slot, u], mb_s[slot, u] = _scores(q_ref[qrows, :], k_ref[krows, :], bias_ref[variant], sink_b)

    def stage2(g, slot, u):
        qrows, krows, _ = geometry(g, u)
        acc, m, l_swapped = _values(p_s[slot, u], mb_s[slot, u], v_ref[krows, :])
        l = pltpu.roll(l_swapped, HEAD_DIM, 1) + jnp.exp2(sink_pair - m)
        o_ref[qrows, :] = ((acc / l) * g_ref[qrows, :]).astype(o_ref.dtype)

    _software_pipeline(list(range(B_STEP // B_TQ // nb)), nb, stage1, stage2)


def _mixer_b(sink2, qb, kb, vb, gb, bias):
    b, s, _ = qb.shape
    nst = s // B_STEP
    ntile = B_WIDTH // LANES
    pairs_per_kv = B_GROUP // 2
    tile = pl.BlockSpec((None, B_STEP, LANES), lambda bi, t, st, sk: (bi, st, t))
    full = pl.BlockSpec((None, s, LANES), lambda bi, t, st, sk: (bi, 0, t // pairs_per_kv))
    return pl.pallas_call(
        functools.partial(_mixer_b_kernel, seq=s),
        grid_spec=pltpu.PrefetchScalarGridSpec(
            num_scalar_prefetch=1,
            grid=(b, ntile, nst),
            in_specs=[tile, full, full, tile,
                      pl.BlockSpec((3, 2 * B_TQ, B_TK), lambda bi, t, st, sk: (0, 0, 0))],
            out_specs=tile,
            scratch_shapes=[pltpu.VMEM((2, B_BLOCKS_PER_GROUP, 2 * B_TQ, B_TK), bf16),
                            pltpu.VMEM((2, B_BLOCKS_PER_GROUP, 2 * B_TQ, LANES), f32)],
        ),
        out_shape=jax.ShapeDtypeStruct((b, s, B_WIDTH), bf16),
        compiler_params=pltpu.CompilerParams(
            dimension_semantics=("parallel", "parallel", "arbitrary"), vmem_limit_bytes=VMEM_LIMIT),
        name="mixer_b",
    )(sink2, qb, kb, vb, gb, bias)


def _out_kernel(ya_ref, yb_ref, pa_ref, pb_ref, x_ref, wa_ref, wb_ref, wo_ref, lg_ref, lb_ref, o_ref):
    sub = x_ref.shape[0] // OUT_SUBTILES
    for i in range(OUT_SUBTILES):
        rows = pl.ds(i * sub, sub)
        bra = jnp.dot(ya_ref[rows, :], wa_ref[...], preferred_element_type=f32)
        brb = jnp.dot(yb_ref[rows, :], wb_ref[...], preferred_element_type=f32)
        merged = pa_ref[rows, :].astype(f32) * bra + pb_ref[rows, :].astype(f32) * brb
        out = jnp.dot(merged.astype(bf16), wo_ref[...], preferred_element_type=f32)
        z = DEEPNORM_ALPHA * x_ref[rows, :] + out
        mu = jnp.mean(z, axis=-1, keepdims=True)
        zc = z - mu
        var = jnp.mean(zc * zc, axis=-1, keepdims=True)
        o_ref[rows, :] = zc * lax.rsqrt(var + LN_EPS) * lg_ref[...] + lb_ref[...]


def _finish(ya, yb, pa, pb, x, wa, wb, wo, lg, lb, tm=512):
    b, s, _ = x.shape
    tok = lambda n: pl.BlockSpec((None, tm, n), lambda bi, i: (bi, i, 0))
    const = lambda r, c: pl.BlockSpec((r, c), lambda bi, i: (0, 0))
    return pl.pallas_call(
        _out_kernel,
        grid=(b, s // tm),
        in_specs=[tok(A_WIDTH), tok(B_WIDTH), tok(D_MODEL), tok(D_MODEL), tok(D_MODEL),
                  const(A_WIDTH, D_MODEL), const(B_WIDTH, D_MODEL), const(D_MODEL, D_MODEL),
                  const(1, D_MODEL), const(1, D_MODEL)],
        out_specs=tok(D_MODEL),
        out_shape=jax.ShapeDtypeStruct((b, s, D_MODEL), f32),
        compiler_params=pltpu.CompilerParams(
            dimension_semantics=("parallel", "parallel"), vmem_limit_bytes=VMEM_LIMIT),
        name="out_proj",
    )(ya, yb, pa, pb, x, wa, wb, wo, lg, lb)


def _encoder_layer(x, w_p, b_gate, sink2, wa, wb, wo, lg, lb, bias_a, bias_b):
    s = x.shape[1]
    assert s % A_SUPER == 0 and s // max(A_DILATIONS) >= A_TK and s % B_STEP == 0 and s >= B_TK
    tables = _rope_tables(s)
    qa, ka, va, ga, qb, kb, vb, gb, pa, pb = _project(x, w_p, b_gate, tables)
    ya = _mixer_a(qa, ka, va, ga, bias_a)
    yb = _mixer_b(sink2, qb, kb, vb, gb, bias_b)
    return _finish(ya, yb, pa, pb, x, wa, wb, wo, lg, lb)


def kernel(x_prompt, x_sample, w_in, b_gate, sink_logit, w_branch_a, w_branch_b, w_out, ln_gain, ln_bias):
    assert w_in.shape[0] == DEPTH
    bias_a = jnp.asarray(_band_bias(A_TQ, A_TK, A_HALF_WINDOW))
    bias_b = jnp.asarray(_band_bias(B_TQ, B_TK, B_HALF_WINDOW))
    y_prompt, y_sample = x_prompt, x_sample
    for l in range(DEPTH):
        w_p = jnp.take(w_in[l], jnp.asarray(PROJ_COLS), axis=1).astype(bf16)
        params = (w_p, b_gate[l], sink_logit[l] * LOG2E,
                  w_branch_a[l].astype(bf16), w_branch_b[l].astype(bf16), w_out[l].astype(bf16),
                  ln_gain[l][None, :], ln_bias[l][None, :], bias_a, bias_b)
        y_prompt = _encoder_layer(y_prompt, *params)
        y_sample = _encoder_layer(y_sample, *params)
    return (y_prompt, y_sample)
```

```python
import functools
import math

import numpy as np
import jax
import jax.numpy as jnp
from jax import lax
from jax.experimental import pallas as pl
from jax.experimental.pallas import tpu as pltpu

D_MODEL = 1024
HEAD_DIM = 64
HALF_DIM = HEAD_DIM // 2
A_HEADS = 8
A_WIDTH = A_HEADS * HEAD_DIM
A_DILATIONS = (1, 4, 16)
A_HALF_WINDOW = 64
B_HEADS = 8
B_KV_HEADS = 2
B_GROUP = B_HEADS // B_KV_HEADS
B_WIDTH = B_HEADS * HEAD_DIM
B_KV_WIDTH = B_KV_HEADS * HEAD_DIM
B_HALF_WINDOW = 128
ROPE_THETA = 10000.0
LN_EPS = 1e-5
NEG_INF = -1e30
DEPTH = 1
DEEPNORM_ALPHA = (2.0 * DEPTH) ** 0.25
IN_SPLIT_SIZES = (A_WIDTH, A_WIDTH, A_WIDTH, A_WIDTH, B_WIDTH, B_KV_WIDTH, B_KV_WIDTH, B_WIDTH, D_MODEL, D_MODEL)
IN_OFFSETS = tuple(int(v) for v in np.cumsum((0,) + IN_SPLIT_SIZES[:-1]))

LANES = 128
LOG2E = math.log2(math.e)
QK_SCALE = LOG2E / math.sqrt(HEAD_DIM)

A_TQ = 128
A_TK = A_TQ + 2 * A_HALF_WINDOW
A_SUPER = A_TQ * max(A_DILATIONS)
B_TQ = 128
B_TK = B_TQ + 2 * B_HALF_WINDOW
B_STEP = 2048
A_MID = A_DILATIONS[1]
assert A_DILATIONS == (1, A_MID, A_MID * A_MID)
OUT_SUBTILES = 4
B_BLOCKS_PER_GROUP = 4

VMEM_LIMIT = 56 * 1024 * 1024

f32 = jnp.float32
bf16 = jnp.bfloat16


B_PAIRS = B_HEADS // 2
assert B_KV_HEADS == 2


def _pair_tiles(w):
    d, t = w.shape[0], w.shape[1]
    return w.reshape(d, t, 2, 2, HALF_DIM).transpose(0, 1, 3, 2, 4).reshape(d, t * LANES)


def _layout_w_in(w_in):
    d = w_in.shape[0]
    seg = [w_in[:, o:o + n] for o, n in zip(IN_OFFSETS, IN_SPLIT_SIZES)]
    qa = _pair_tiles(seg[0].reshape(d, A_HEADS // 2, 2, HEAD_DIM))
    ka = _pair_tiles(seg[1].reshape(d, A_HEADS // 2, 2, HEAD_DIM))
    by_tile = lambda w: w.reshape(d, B_KV_HEADS, B_PAIRS, HEAD_DIM).transpose(0, 2, 1, 3)
    qb = _pair_tiles(by_tile(seg[4]))
    kb = _pair_tiles(seg[5].reshape(d, 1, B_KV_HEADS, HEAD_DIM))
    gb = by_tile(seg[7]).reshape(d, B_WIDTH)
    return jnp.concatenate([qa, ka, seg[2], seg[3], qb, kb, seg[6], gb, seg[8], seg[9]], axis=1)


def _layout_w_branch_b(w):
    return w.reshape(B_KV_HEADS, B_PAIRS, HEAD_DIM, -1).transpose(1, 0, 2, 3).reshape(w.shape)


P_QA, P_KA, P_VA, P_GA = 0, 512, 1024, 1536
P_QB, P_KB, P_GB = 2048, 2560, 2816
P_PA, P_PB = 3328, 4352
P_TOTAL = 5376
assert P_TOTAL == sum(IN_SPLIT_SIZES)


def _rope_tables(seq):
    inv_freq = ROPE_THETA ** (-jnp.arange(HALF_DIM, dtype=f32) / HALF_DIM)
    ang = jnp.arange(seq, dtype=f32)[:, None] * inv_freq[None, :]
    cos = jnp.tile(jnp.cos(ang), (1, 4))
    sin = jnp.tile(jnp.sin(ang), (1, 4))
    sign = jnp.where(jnp.arange(LANES) < 2 * HALF_DIM, -1.0, 1.0).astype(f32)
    sin = sin * sign[None, :]
    return cos * QK_SCALE, sin * QK_SCALE, cos, sin


def _band_bias(tq, tk, half_window):
    i = np.arange(tq)[:, None]
    j = np.arange(tk)[None, :]
    out = []
    for shift in (0, half_window, tk - tq):
        ok = np.abs(j - shift - i) <= half_window
        out.append(np.where(ok, 0.0, NEG_INF).astype(np.float32))
    return np.stack(out, 0)


def _proj_kernel(x_ref, w_ref, bg_ref, cq_ref, sq_ref, ck_ref, sk_ref,
                 qa_o, ka_o, va_o, ga_o, qb_o, kb_o, vb_o, gb_o, pa_o, pb_o):
    xb = x_ref[...].astype(bf16)

    def proj(c0, n):
        return jnp.dot(xb, w_ref[:, c0:c0 + n], preferred_element_type=f32)

    def rope(h, c, s):
        tiles = []
        for t in range(h.shape[1] // LANES):
            ht = h[:, t * LANES:(t + 1) * LANES]
            tiles.append(ht * c + pltpu.roll(ht, 2 * HALF_DIM, 1) * s)
        return jnp.concatenate(tiles, axis=1)

    cq, sq, ck, sk = cq_ref[...], sq_ref[...], ck_ref[...], sk_ref[...]
    pa_o[...] = jax.nn.sigmoid(proj(P_PA, D_MODEL) + bg_ref[0:1, :]).astype(pa_o.dtype)
    pb_o[...] = jax.nn.sigmoid(proj(P_PB, D_MODEL) + bg_ref[1:2, :]).astype(pb_o.dtype)
    ga_o[...] = jax.nn.silu(proj(P_GA, A_WIDTH))
    gb_o[...] = jax.nn.silu(proj(P_GB, B_WIDTH))
    qa_o[...] = rope(proj(P_QA, A_WIDTH), cq, sq)
    ka_o[...] = rope(proj(P_KA, A_WIDTH), ck, sk)
    qb_o[...] = rope(proj(P_QB, B_WIDTH), cq, sq)
    kvb = proj(P_KB, 2 * B_KV_WIDTH)
    kb_o[...] = rope(kvb[:, :B_KV_WIDTH], ck, sk)
    vb_o[...] = kvb[:, B_KV_WIDTH:]
    va_o[...] = proj(P_VA, A_WIDTH)


def _project(x, w_p, b_gate, tables, tm=256):
    b, s, _ = x.shape
    nt = s // tm
    widths = (A_WIDTH, A_WIDTH, A_WIDTH, A_WIDTH, B_WIDTH, B_KV_WIDTH, B_KV_WIDTH, B_WIDTH, D_MODEL, D_MODEL)
    dtypes = (f32,) * 8 + (bf16, bf16)
    tok = lambda n: pl.BlockSpec((None, tm, n), lambda bi, i: (bi, i, 0))
    tab = pl.BlockSpec((tm, LANES), lambda bi, i: (i, 0))
    return pl.pallas_call(
        _proj_kernel,
        grid=(b, nt),
        in_specs=[tok(D_MODEL),
                  pl.BlockSpec((D_MODEL, P_TOTAL), lambda bi, i: (0, 0)),
                  pl.BlockSpec((2, D_MODEL), lambda bi, i: (0, 0)),
                  tab, tab, tab, tab],
        out_specs=[tok(n) for n in widths],
        out_shape=[jax.ShapeDtypeStruct((b, s, n), dt) for n, dt in zip(widths, dtypes)],
        compiler_params=pltpu.CompilerParams(
            dimension_semantics=("parallel", "parallel"), vmem_limit_bytes=VMEM_LIMIT),
        name="in_proj",
    )(x, w_p, b_gate, *tables)


def _pair_slot0():
    lane = lax.broadcasted_iota(jnp.int32, (1, LANES), 1)
    return (lane % HEAD_DIM) < HALF_DIM


def _exp_rows(s, mb):
    return [jnp.exp2(s[:, c * LANES:(c + 1) * LANES] - mb).astype(bf16) for c in range(s.shape[1] // LANES)]


def _row_max(s):
    return jnp.broadcast_to(jnp.max(s, axis=-1, keepdims=True), (s.shape[0], LANES))


def _scores_head_stacked(qf, kf, bias):
    tq = qf.shape[0]
    slot0 = _pair_slot0()
    qb = qf.astype(bf16)
    zero = jnp.zeros_like(qb)
    q2 = jnp.concatenate([jnp.where(slot0, qb, zero), jnp.where(slot0, zero, qb)], axis=0)
    s = lax.dot_general(q2, kf.astype(bf16), (((1,), (1,)), ((), ())), preferred_element_type=f32)
    s0, s1 = s[:tq] + bias, s[tq:] + bias
    m0, m1 = _row_max(s0), _row_max(s1)
    return jnp.concatenate(_exp_rows(s0, m0) + _exp_rows(s1, m1), axis=1), jnp.concatenate([m0, m1], axis=0)


def _scores_head_blocked(qf, kf, bias, sink0, sink1):
    tk = kf.shape[0]
    slot0 = _pair_slot0()
    kb = kf.astype(bf16)
    zero = jnp.zeros_like(kb)
    k2 = jnp.concatenate([jnp.where(slot0, kb, zero), jnp.where(slot0, zero, kb)], axis=0)
    s = lax.dot_general(qf.astype(bf16), k2, (((1,), (1,)), ((), ())), preferred_element_type=f32)
    s0, s1 = s[:, :tk] + bias, s[:, tk:] + bias
    m0, m1 = jnp.maximum(_row_max(s0), sink0), jnp.maximum(_row_max(s1), sink1)
    return jnp.concatenate(_exp_rows(s0, m0) + _exp_rows(s1, m1), axis=1), jnp.concatenate([m0, m1], axis=0)


def _values(p, mb, vf):
    tq = p.shape[0]
    key_lane = lax.broadcasted_iota(jnp.int32, vf.shape, 1)
    sel0 = jnp.where(key_lane < HEAD_DIM, 1.0, 0.0).astype(bf16)
    sel1 = jnp.where(key_lane < HEAD_DIM, 0.0, 1.0).astype(bf16)
    vb = vf.astype(bf16)
    w = jnp.concatenate([jnp.concatenate([vb * sel0, sel0], axis=1),
                         jnp.concatenate([vb * sel1, sel1], axis=1)], axis=0)
    o = jnp.dot(p, w, preferred_element_type=f32)
    first = lax.broadcasted_iota(jnp.int32, (tq, LANES), 1) < HEAD_DIM
    return o[:, :LANES], jnp.where(first, mb[:tq], mb[tq:]), o[:, LANES:]


def _aligned(x, multiple):
    if isinstance(x, int):
        assert x % multiple == 0
        return x
    return pl.multiple_of(x, multiple)


def _window(q0, length, tq, tk, half_window):
    w0 = jnp.clip(q0 - half_window, 0, length - tk)
    variant = jnp.where(q0 == 0, 0, jnp.where(q0 == length - tq, 2, 1))
    return w0, variant


def _software_pipeline(groups, nb, stage1, stage2):
    for u in range(nb):
        stage1(groups[0], 0, u)
    for i in range(1, len(groups)):
        for u in range(nb):
            stage1(groups[i], i % 2, u)
        for u in range(nb):
            stage2(groups[i - 1], (i - 1) % 2, u)
    for u in range(nb):
        stage2(groups[-1], (len(groups) - 1) % 2, u)


def _mixer_a_kernel(q_ref, k_ref, v_ref, g_ref, bias_ref, o_ref,
                    k4_s, v4_s, q4_s, p_s, mb_s, acc_s, m_s, l_s, out_s, *, seq):
    st = pl.program_id(2)
    cls_len = seq // A_MID
    cls_rows = A_SUPER // A_MID

    @pl.when(st == 0)
    def _():
        def fill(c, carry):
            dst = pl.ds(_aligned(c * cls_rows, cls_rows), cls_rows)
            for r in range(A_MID):
                src = pl.ds(_aligned(c * A_SUPER, A_SUPER) + r, cls_rows, stride=A_MID)
                k4_s[r, dst, :] = k_ref[src, :]
                v4_s[r, dst, :] = v_ref[src, :]
            return carry
        lax.fori_loop(0, seq // A_SUPER, fill, 0)

    for r in range(A_MID):
        q4_s[r] = q_ref[pl.ds(r, cls_rows, stride=A_MID), :]

    nb = A_MID
    far_w0, far_variant = _window(st * A_TQ, seq // A_DILATIONS[2], A_TQ, A_TK, A_HALF_WINDOW)
    far_kbase = _aligned(far_w0 * A_MID, A_HALF_WINDOW)

    def geometry(kind, g, u):
        if kind == 0:
            blk = g * nb + u
            w0, variant = _window(st * A_SUPER + blk * A_TQ, seq, A_TQ, A_TK, A_HALF_WINDOW)
            qrows = pl.ds(_aligned(blk * A_TQ, A_TQ), A_TQ)
            krows = pl.ds(_aligned(w0, A_HALF_WINDOW), A_TK)
            return (lambda: q_ref[qrows, :]), krows, k_ref, v_ref, variant, qrows
        if kind == 1:
            w0, variant = _window(st * cls_rows + g * A_TQ, cls_len, A_TQ, A_TK, A_HALF_WINDOW)
            qrows = pl.ds(_aligned(g * A_TQ, A_TQ), A_TQ)
            krows = pl.ds(_aligned(w0, A_HALF_WINDOW), A_TK)
            out_rows = pl.ds(_aligned(u * cls_rows + g * A_TQ, A_TQ), A_TQ)
            return (lambda: q4_s[u, qrows, :]), krows, k4_s.at[u], v4_s.at[u], variant, out_rows
        qrows = pl.ds(g, A_TQ, stride=A_MID)
        krows = pl.ds(far_kbase + g, A_TK, stride=A_MID)
        out_rows = pl.ds(u * cls_rows + g, A_TQ, stride=A_MID)
        return (lambda: q4_s[u, qrows, :]), krows, k4_s.at[u], v4_s.at[u], far_variant, out_rows

    def stage1(group, slot, u):
        kind, g = group
        load_q, krows, kk, _, variant, _ = geometry(kind, g, u)
        p_s[slot, u], mb_s[slot, u] = _scores_head_stacked(load_q(), kk[krows, :], bias_ref[variant])

    def stage2(group, slot, u):
        kind, g = group
        _, krows, _, vv, _, out_rows = geometry(kind, g, u)
        acc_s[kind, out_rows, :], m_s[kind, out_rows, :], l_s[kind, out_rows, :] = _values(
            p_s[slot, u], mb_s[slot, u], vv[krows, :])

    groups = [(kind, g) for kind in range(len(A_DILATIONS)) for g in range(A_SUPER // A_TQ // nb)]
    _software_pipeline(groups, nb, stage1, stage2)

    for r in range(A_MID):
        tok = pl.ds(r, cls_rows, stride=A_MID)
        cls = pl.ds(r * cls_rows, cls_rows)
        rows = (tok, cls, cls)
        ms = [m_s[gi, rows[gi], :] for gi in range(3)]
        m_all = jnp.maximum(jnp.maximum(ms[0], ms[1]), ms[2])
        num = jnp.zeros((cls_rows, LANES), f32)
        den = jnp.zeros((cls_rows, LANES), f32)
        for gi in range(3):
            w = jnp.exp2(ms[gi] - m_all)
            num = num + w * acc_s[gi, rows[gi], :]
            den = den + w * l_s[gi, rows[gi], :]
        out_s[tok, :] = (num / den) * g_ref[tok, :]
    o_ref[...] = out_s[...].astype(o_ref.dtype)


def _mixer_a(qa, ka, va, ga, bias):
    b, s, _ = qa.shape
    nst = s // A_SUPER
    ntile = A_WIDTH // LANES
    tile = pl.BlockSpec((None, A_SUPER, LANES), lambda bi, t, st: (bi, st, t))
    full = pl.BlockSpec((None, s, LANES), lambda bi, t, st: (bi, 0, t))
    groups = pltpu.VMEM((len(A_DILATIONS), A_SUPER, LANES), f32)
    return pl.pallas_call(
        functools.partial(_mixer_a_kernel, seq=s),
        grid=(b, ntile, nst),
        in_specs=[tile, full, full, tile,
                  pl.BlockSpec((3, A_TQ, A_TK), lambda bi, t, st: (0, 0, 0))],
        out_specs=tile,
        out_shape=jax.ShapeDtypeStruct((b, s, A_WIDTH), bf16),
        scratch_shapes=[pltpu.VMEM((A_MID, s // A_MID, LANES), f32),
                        pltpu.VMEM((A_MID, s // A_MID, LANES), f32),
                        pltpu.VMEM((A_MID, A_SUPER // A_MID, LANES), f32),
                        pltpu.VMEM((2, A_MID, A_TQ, 2 * A_TK), bf16),
                        pltpu.VMEM((2, A_MID, 2 * A_TQ, LANES), f32),
                        groups, groups, groups,
                        pltpu.VMEM((A_SUPER, LANES), f32)],
        compiler_params=pltpu.CompilerParams(
            dimension_semantics=("arbitrary", "arbitrary", "arbitrary"), vmem_limit_bytes=VMEM_LIMIT),
        name="mixer_a",
    )(qa, ka, va, ga, bias)


def _mixer_b_kernel(sink_ref, q_ref, k_ref, v_ref, g_ref, bias_ref, o_ref, p_s, mb_s, *, seq):
    t = pl.program_id(1)
    st = pl.program_id(2)
    nb = B_BLOCKS_PER_GROUP
    first = lax.broadcasted_iota(jnp.int32, (B_TQ, LANES), 1) < HEAD_DIM
    sink0 = jnp.full((B_TQ, LANES), sink_ref[t], f32)
    sink1 = jnp.full((B_TQ, LANES), sink_ref[t + B_GROUP], f32)
    sink_pair = jnp.where(first, sink0, sink1)

    def geometry(g, u):
        blk = g * nb + u
        w0, variant = _window(st * B_STEP + blk * B_TQ, seq, B_TQ, B_TK, B_HALF_WINDOW)
        qrows = pl.ds(_aligned(blk * B_TQ, B_TQ), B_TQ)
        krows = pl.ds(_aligned(w0, B_HALF_WINDOW), B_TK)
        return qrows, krows, variant

    def stage1(g, slot, u):
        qrows, krows, variant = geometry(g, u)
        p_s[slot, u], mb_s[slot, u] = _scores_head_blocked(q_ref[qrows, :], k_ref[krows, :], bias_ref[variant],
                                                          sink0, sink1)

    def stage2(g, slot, u):
        qrows, krows, _ = geometry(g, u)
        acc, m, l = _values(p_s[slot, u], mb_s[slot, u], v_ref[krows, :])
        l = l + jnp.exp2(sink_pair - m)
        o_ref[qrows, :] = ((acc / l) * g_ref[qrows, :]).astype(o_ref.dtype)

    _software_pipeline(list(range(B_STEP // B_TQ // nb)), nb, stage1, stage2)


def _mixer_b(sink2, qb, kb, vb, gb, bias):
    b, s, _ = qb.shape
    nst = s // B_STEP
    ntile = B_WIDTH // LANES
    tile = pl.BlockSpec((None, B_STEP, LANES), lambda bi, t, st, sk: (bi, st, t))
    full = pl.BlockSpec((None, s, LANES), lambda bi, t, st, sk: (bi, 0, 0))
    return pl.pallas_call(
        functools.partial(_mixer_b_kernel, seq=s),
        grid_spec=pltpu.PrefetchScalarGridSpec(
            num_scalar_prefetch=1,
            grid=(b, ntile, nst),
            in_specs=[tile, full, full, tile,
                      pl.BlockSpec((3, B_TQ, B_TK), lambda bi, t, st, sk: (0, 0, 0))],
            out_specs=tile,
            scratch_shapes=[pltpu.VMEM((2, B_BLOCKS_PER_GROUP, B_TQ, 2 * B_TK), bf16),
                            pltpu.VMEM((2, B_BLOCKS_PER_GROUP, 2 * B_TQ, LANES), f32)],
        ),
        out_shape=jax.ShapeDtypeStruct((b, s, B_WIDTH), bf16),
        compiler_params=pltpu.CompilerParams(
            dimension_semantics=("parallel", "parallel", "arbitrary"), vmem_limit_bytes=VMEM_LIMIT),
        name="mixer_b",
    )(sink2, qb, kb, vb, gb, bias)


def _out_kernel(ya_ref, yb_ref, pa_ref, pb_ref, x_ref, wa_ref, wb_ref, wo_ref, lg_ref, lb_ref, o_ref):
    sub = x_ref.shape[0] // OUT_SUBTILES
    for i in range(OUT_SUBTILES):
        rows = pl.ds(i * sub, sub)
        bra = jnp.dot(ya_ref[rows, :], wa_ref[...], preferred_element_type=f32)
        brb = jnp.dot(yb_ref[rows, :], wb_ref[...], preferred_element_type=f32)
        merged = pa_ref[rows, :].astype(f32) * bra + pb_ref[rows, :].astype(f32) * brb
        out = jnp.dot(merged.astype(bf16), wo_ref[...], preferred_element_type=f32)
        z = DEEPNORM_ALPHA * x_ref[rows, :] + out
        mu = jnp.mean(z, axis=-1, keepdims=True)
        zc = z - mu
        var = jnp.mean(zc * zc, axis=-1, keepdims=True)
        o_ref[rows, :] = zc * lax.rsqrt(var + LN_EPS) * lg_ref[...] + lb_ref[...]


def _finish(ya, yb, pa, pb, x, wa, wb, wo, lg, lb, tm=1024):
    b, s, _ = x.shape
    tok = lambda n: pl.BlockSpec((None, tm, n), lambda bi, i: (bi, i, 0))
    const = lambda r, c: pl.BlockSpec((r, c), lambda bi, i: (0, 0))
    return pl.pallas_call(
        _out_kernel,
        grid=(b, s // tm),
        in_specs=[tok(A_WIDTH), tok(B_WIDTH), tok(D_MODEL), tok(D_MODEL), tok(D_MODEL),
                  const(A_WIDTH, D_MODEL), const(B_WIDTH, D_MODEL), const(D_MODEL, D_MODEL),
                  const(1, D_MODEL), const(1, D_MODEL)],
        out_specs=tok(D_MODEL),
        out_shape=jax.ShapeDtypeStruct((b, s, D_MODEL), f32),
        compiler_params=pltpu.CompilerParams(
            dimension_semantics=("parallel", "parallel"), vmem_limit_bytes=VMEM_LIMIT),
        name="out_proj",
    )(ya, yb, pa, pb, x, wa, wb, wo, lg, lb)


def _encoder_layer(x, w_p, b_gate, sink2, wa, wb, wo, lg, lb, bias_a, bias_b):
    s = x.shape[1]
    assert s % A_SUPER == 0 and s // max(A_DILATIONS) >= A_TK and s % B_STEP == 0 and s >= B_TK
    tables = _rope_tables(s)
    qa, ka, va, ga, qb, kb, vb, gb, pa, pb = _project(x, w_p, b_gate, tables)
    ya = _mixer_a(qa, ka, va, ga, bias_a)
    yb = _mixer_b(sink2, qb, kb, vb, gb, bias_b)
    return _finish(ya, yb, pa, pb, x, wa, wb, wo, lg, lb)


def kernel(x_prompt, x_sample, w_in, b_gate, sink_logit, w_branch_a, w_branch_b, w_out, ln_gain, ln_bias):
    assert w_in.shape[0] == DEPTH
    bias_a = jnp.asarray(_band_bias(A_TQ, A_TK, A_HALF_WINDOW))
    bias_b = jnp.asarray(_band_bias(B_TQ, B_TK, B_HALF_WINDOW))
    y_prompt, y_sample = x_prompt, x_sample
    for l in range(DEPTH):
        w_p = _layout_w_in(w_in[l]).astype(bf16)
        params = (w_p, b_gate[l], sink_logit[l] * LOG2E,
                  w_branch_a[l].astype(bf16), _layout_w_branch_b(w_branch_b[l]).astype(bf16),
                  w_out[l].astype(bf16),
                  ln_gain[l][None, :], ln_bias[l][None, :], bias_a, bias_b)
        y_prompt = _encoder_layer(y_prompt, *params)
        y_sample = _encoder_layer(y_sample, *params)
    return (y_prompt, y_sample)
```

```python
import functools
import math

import numpy as np
import jax
import jax.numpy as jnp
from jax import lax
from jax.experimental import pallas as pl
from jax.experimental.pallas import tpu as pltpu

D_MODEL = 1024
HEAD_DIM = 64
HALF_DIM = HEAD_DIM // 2
A_HEADS = 8
A_WIDTH = A_HEADS * HEAD_DIM
A_DILATIONS = (1, 4, 16)
A_HALF_WINDOW = 64
B_HEADS = 8
B_KV_HEADS = 2
B_GROUP = B_HEADS // B_KV_HEADS
B_WIDTH = B_HEADS * HEAD_DIM
B_KV_WIDTH = B_KV_HEADS * HEAD_DIM
B_HALF_WINDOW = 128
ROPE_THETA = 10000.0
LN_EPS = 1e-5
NEG_INF = -1e30
DEPTH = 1
DEEPNORM_ALPHA = (2.0 * DEPTH) ** 0.25
IN_SPLIT_SIZES = (A_WIDTH, A_WIDTH, A_WIDTH, A_WIDTH, B_WIDTH, B_KV_WIDTH, B_KV_WIDTH, B_WIDTH, D_MODEL, D_MODEL)
IN_OFFSETS = tuple(int(v) for v in np.cumsum((0,) + IN_SPLIT_SIZES[:-1]))

LANES = 128
LOG2E = math.log2(math.e)
QK_SCALE = LOG2E / math.sqrt(HEAD_DIM)

A_TQ = 128
A_TK = A_TQ + 2 * A_HALF_WINDOW
A_SUPER = A_TQ * max(A_DILATIONS)
B_TQ = 128
B_TK = B_TQ + 2 * B_HALF_WINDOW
A_MID = A_DILATIONS[1]
assert A_DILATIONS == (1, A_MID, A_MID * A_MID)
OUT_SUBTILES = 4
A_KIND_ORDER = (0, 2, 1)
A_MERGE_KIND = A_KIND_ORDER[-1]
A_SIDE_SLOT = {0: 0, 2: 1}

VMEM_LIMIT = 56 * 1024 * 1024

f32 = jnp.float32
bf16 = jnp.bfloat16


B_PAIRS = B_HEADS // 2
assert B_KV_HEADS == 2


def _pair_tiles(w):
    d, t = w.shape[0], w.shape[1]
    return w.reshape(d, t, 2, 2, HALF_DIM).transpose(0, 1, 3, 2, 4).reshape(d, t * LANES)


def _layout_w_in(w_in):
    d = w_in.shape[0]
    seg = [w_in[:, o:o + n] for o, n in zip(IN_OFFSETS, IN_SPLIT_SIZES)]
    qa = _pair_tiles(seg[0].reshape(d, A_HEADS // 2, 2, HEAD_DIM))
    ka = _pair_tiles(seg[1].reshape(d, A_HEADS // 2, 2, HEAD_DIM))
    by_tile = lambda w: w.reshape(d, B_KV_HEADS, B_PAIRS, HEAD_DIM).transpose(0, 2, 1, 3)
    qb = _pair_tiles(by_tile(seg[4]))
    kb = _pair_tiles(seg[5].reshape(d, 1, B_KV_HEADS, HEAD_DIM))
    gb = by_tile(seg[7]).reshape(d, B_WIDTH)
    return jnp.concatenate([qa, ka, seg[2], seg[3], qb, kb, seg[6], gb, seg[8], seg[9]], axis=1)


def _layout_w_branch_b(w):
    return w.reshape(B_KV_HEADS, B_PAIRS, HEAD_DIM, -1).transpose(1, 0, 2, 3).reshape(w.shape)


P_QA, P_KA, P_VA, P_GA = 0, 512, 1024, 1536
P_QB, P_KB, P_GB = 2048, 2560, 2816
P_PA, P_PB = 3328, 4352
P_TOTAL = 5376
assert P_TOTAL == sum(IN_SPLIT_SIZES)


def _rope_tables(seq):
    inv_freq = ROPE_THETA ** (-jnp.arange(HALF_DIM, dtype=f32) / HALF_DIM)
    ang = jnp.arange(seq, dtype=f32)[:, None] * inv_freq[None, :]
    cos = jnp.tile(jnp.cos(ang), (1, 4))
    sin = jnp.tile(jnp.sin(ang), (1, 4))
    sign = jnp.where(jnp.arange(LANES) < 2 * HALF_DIM, -1.0, 1.0).astype(f32)
    sin = sin * sign[None, :]
    return cos * QK_SCALE, sin * QK_SCALE, cos, sin


def _band_bias(tq, tk, half_window):
    i = np.arange(tq)[:, None]
    j = np.arange(tk)[None, :]
    out = []
    for shift in (0, half_window, tk - tq):
        ok = np.abs(j - shift - i) <= half_window
        out.append(np.where(ok, 0.0, NEG_INF).astype(np.float32))
    return np.stack(out, 0)


def _halo_bias(tq, tk, half_window):
    i = np.arange(tq)[:, None]
    j = np.arange(tk)[None, :]
    band = np.abs(j - half_window - i) <= half_window
    out = [band & (j >= half_window), band, band & (j < tk - half_window)]
    return np.stack([np.where(ok, 0.0, NEG_INF).astype(np.float32) for ok in out], 0)


def _rope(h, c, s):
    tiles = []
    for t in range(h.shape[1] // LANES):
        ht = h[:, t * LANES:(t + 1) * LANES]
        tiles.append(ht * c + pltpu.roll(ht, 2 * HALF_DIM, 1) * s)
    return jnp.concatenate(tiles, axis=1)


def _proj_kernel(sink_ref, x_ref, xl_ref, xr_ref, w_ref, bg_ref, cq_ref, sq_ref, ck_ref, sk_ref,
                 ckl_ref, skl_ref, ckr_ref, skr_ref, bias_ref,
                 qa_o, ka_o, va_o, ga_o, pa_o, pb_o, yb_o,
                 qb_s, gb_s, kb_s, vb_s, p_s, mb_s):
    i, nt = pl.program_id(1), pl.num_programs(1)
    tm = x_ref.shape[0]
    xb = x_ref[...].astype(bf16)
    cq, sq, ck, sk = cq_ref[...], sq_ref[...], ck_ref[...], sk_ref[...]

    def proj(c0, n, lhs=xb):
        return jnp.dot(lhs, w_ref[:, c0:c0 + n], preferred_element_type=f32)

    x_ext = jnp.concatenate([xl_ref[...].astype(bf16), xb, xr_ref[...].astype(bf16)], axis=0)
    kvb = proj(P_KB, 2 * B_KV_WIDTH, x_ext)
    kb_s[...] = _rope(kvb[:, :B_KV_WIDTH], jnp.concatenate([ckl_ref[...], ck, ckr_ref[...]], axis=0),
                      jnp.concatenate([skl_ref[...], sk, skr_ref[...]], axis=0))
    vb_s[...] = kvb[:, B_KV_WIDTH:]
    qb_s[...] = _rope(proj(P_QB, B_WIDTH), cq, sq)
    gb_s[...] = jax.nn.silu(proj(P_GB, B_WIDTH))

    nb = tm // B_TQ
    first = lax.broadcasted_iota(jnp.int32, (B_TQ, LANES), 1) < HEAD_DIM

    def stage1(t, slot, u):
        rows, cols = pl.ds(u * B_TQ, B_TQ), pl.ds(t * LANES, LANES)
        variant = jnp.where(i == 0, 0, 1) if u == 0 else jnp.where(i == nt - 1, 2, 1) if u == nb - 1 else 1
        sink0 = jnp.full((B_TQ, LANES), sink_ref[t], f32)
        sink1 = jnp.full((B_TQ, LANES), sink_ref[t + B_GROUP], f32)
        p_s[slot, u], mb_s[slot, u] = _scores_head_blocked(qb_s[rows, cols], kb_s[pl.ds(u * B_TQ, B_TK), :],
                                                          bias_ref[variant], sink0, sink1)

    def stage2(t, slot, u):
        rows, cols = pl.ds(u * B_TQ, B_TQ), pl.ds(t * LANES, LANES)
        acc, m, l = _values(p_s[slot, u], mb_s[slot, u], vb_s[pl.ds(u * B_TQ, B_TK), :])
        l = l + jnp.exp2(jnp.where(first, sink_ref[t], sink_ref[t + B_GROUP]) - m)
        yb_o[rows, cols] = ((acc / l) * gb_s[rows, cols]).astype(yb_o.dtype)

    def seg_pa():
        pa_o[...] = jax.nn.sigmoid(proj(P_PA, D_MODEL) + bg_ref[0:1, :]).astype(pa_o.dtype)

    def seg_pb():
        pb_o[...] = jax.nn.sigmoid(proj(P_PB, D_MODEL) + bg_ref[1:2, :]).astype(pb_o.dtype)

    def seg_ga():
        ga_o[...] = jax.nn.silu(proj(P_GA, A_WIDTH))

    def seg_qa():
        qa_o[...] = _rope(proj(P_QA, A_WIDTH), cq, sq)

    def seg_ka():
        ka_o[...] = _rope(proj(P_KA, A_WIDTH), ck, sk)

    _software_pipeline(list(range(B_PAIRS)), nb, stage1, stage2, between=(seg_pa, seg_pb, seg_ga, seg_qa, seg_ka))
    va_o[...] = proj(P_VA, A_WIDTH)


def _project_and_mix_b(x, w_p, b_gate, sink2, tables, bias_b, tm=512):
    b, s, _ = x.shape
    halo = B_HALF_WINDOW
    per_tile, last = tm // halo, s // halo - 1
    tok = lambda n: pl.BlockSpec((None, tm, n), lambda bi, i, sk: (bi, i, 0))
    left = lambda i: jnp.maximum(i * per_tile - 1, 0)
    right = lambda i: jnp.minimum((i + 1) * per_tile, last)
    tab = pl.BlockSpec((tm, LANES), lambda bi, i, sk: (i, 0))
    tab_l = pl.BlockSpec((halo, LANES), lambda bi, i, sk: (left(i), 0))
    tab_r = pl.BlockSpec((halo, LANES), lambda bi, i, sk: (right(i), 0))
    cq, sq, ck, sk_ = tables
    widths = (A_WIDTH, A_WIDTH, A_WIDTH, A_WIDTH, D_MODEL, D_MODEL, B_WIDTH)
    dtypes = (f32, f32, f32, f32, bf16, bf16, bf16)
    return pl.pallas_call(
        _proj_kernel,
        grid_spec=pltpu.PrefetchScalarGridSpec(
            num_scalar_prefetch=1,
            grid=(b, s // tm),
            in_specs=[tok(D_MODEL),
                      pl.BlockSpec((None, halo, D_MODEL), lambda bi, i, sk: (bi, left(i), 0)),
                      pl.BlockSpec((None, halo, D_MODEL), lambda bi, i, sk: (bi, right(i), 0)),
                      pl.BlockSpec((D_MODEL, P_TOTAL), lambda bi, i, sk: (0, 0)),
                      pl.BlockSpec((2, D_MODEL), lambda bi, i, sk: (0, 0)),
                      tab, tab, tab, tab, tab_l, tab_l, tab_r, tab_r,
                      pl.BlockSpec((3, B_TQ, B_TK), lambda bi, i, sk: (0, 0, 0))],
            out_specs=[tok(n) for n in widths],
            scratch_shapes=[pltpu.VMEM((tm, B_WIDTH), f32), pltpu.VMEM((tm, B_WIDTH), f32),
                            pltpu.VMEM((tm + 2 * halo, B_KV_WIDTH), f32), pltpu.VMEM((tm + 2 * halo, B_KV_WIDTH), f32),
                            pltpu.VMEM((2, tm // B_TQ, B_TQ, 2 * B_TK), bf16),
                            pltpu.VMEM((2, tm // B_TQ, 2 * B_TQ, LANES), f32)],
        ),
        out_shape=[jax.ShapeDtypeStruct((b, s, n), dt) for n, dt in zip(widths, dtypes)],
        compiler_params=pltpu.CompilerParams(
            dimension_semantics=("parallel", "parallel"), vmem_limit_bytes=VMEM_LIMIT),
        name="in_proj_mixer_b",
    )(sink2, x, x, x, w_p, b_gate, cq, sq, ck, sk_, ck, sk_, ck, sk_, bias_b)


def _pair_slot0():
    lane = lax.broadcasted_iota(jnp.int32, (1, LANES), 1)
    return (lane % HEAD_DIM) < HALF_DIM


def _exp_rows(s, mb):
    return [jnp.exp2(s[:, c * LANES:(c + 1) * LANES] - mb).astype(bf16) for c in range(s.shape[1] // LANES)]


def _row_max(s):
    return jnp.broadcast_to(jnp.max(s, axis=-1, keepdims=True), (s.shape[0], LANES))


def _scores_head_stacked(qf, kf, bias):
    tq = qf.shape[0]
    slot0 = _pair_slot0()
    qb = qf.astype(bf16)
    zero = jnp.zeros_like(qb)
    q2 = jnp.concatenate([jnp.where(slot0, qb, zero), jnp.where(slot0, zero, qb)], axis=0)
    s = lax.dot_general(q2, kf.astype(bf16), (((1,), (1,)), ((), ())), preferred_element_type=f32)
    s0, s1 = s[:tq] + bias, s[tq:] + bias
    m0, m1 = _row_max(s0), _row_max(s1)
    return jnp.concatenate(_exp_rows(s0, m0) + _exp_rows(s1, m1), axis=1), jnp.concatenate([m0, m1], axis=0)


def _scores_head_blocked(qf, kf, bias, sink0, sink1):
    tk = kf.shape[0]
    slot0 = _pair_slot0()
    kb = kf.astype(bf16)
    zero = jnp.zeros_like(kb)
    k2 = jnp.concatenate([jnp.where(slot0, kb, zero), jnp.where(slot0, zero, kb)], axis=0)
    s = lax.dot_general(qf.astype(bf16), k2, (((1,), (1,)), ((), ())), preferred_element_type=f32)
    s0, s1 = s[:, :tk] + bias, s[:, tk:] + bias
    m0, m1 = jnp.maximum(_row_max(s0), sink0), jnp.maximum(_row_max(s1), sink1)
    return jnp.concatenate(_exp_rows(s0, m0) + _exp_rows(s1, m1), axis=1), jnp.concatenate([m0, m1], axis=0)


def _values(p, mb, vf):
    tq = p.shape[0]
    key_lane = lax.broadcasted_iota(jnp.int32, vf.shape, 1)
    sel0 = jnp.where(key_lane < HEAD_DIM, 1.0, 0.0).astype(bf16)
    sel1 = jnp.where(key_lane < HEAD_DIM, 0.0, 1.0).astype(bf16)
    vb = vf.astype(bf16)
    w = jnp.concatenate([jnp.concatenate([vb * sel0, sel0], axis=1),
                         jnp.concatenate([vb * sel1, sel1], axis=1)], axis=0)
    o = jnp.dot(p, w, preferred_element_type=f32)
    first = lax.broadcasted_iota(jnp.int32, (tq, LANES), 1) < HEAD_DIM
    return o[:, :LANES], jnp.where(first, mb[:tq], mb[tq:]), o[:, LANES:]


def _aligned(x, multiple):
    if isinstance(x, int):
        assert x % multiple == 0
        return x
    return pl.multiple_of(x, multiple)


def _window(q0, length, tq, tk, half_window):
    w0 = jnp.clip(q0 - half_window, 0, length - tk)
    variant = jnp.where(q0 == 0, 0, jnp.where(q0 == length - tq, 2, 1))
    return w0, variant


def _software_pipeline(groups, nb, stage1, stage2, between=()):
    extra = list(between) + [None] * (len(groups) + 1 - len(between))
    for i in range(len(groups) + 1):
        if extra[i] is not None:
            extra[i]()
        if i < len(groups):
            for u in range(nb):
                stage1(groups[i], i % 2, u)
        if i > 0:
            for u in range(nb):
                stage2(groups[i - 1], (i - 1) % 2, u)


def _mixer_a_kernel(q_ref, k_ref, v_ref, g_ref, bias_ref, o_ref,
                    k4_s, v4_s, q4_s, p_s, mb_s, acc_s, m_s, l_s, out_s, *, seq):
    st = pl.program_id(2)
    cls_len = seq // A_MID
    cls_rows = A_SUPER // A_MID

    @pl.when(st == 0)
    def _():
        def fill(c, carry):
            dst = pl.ds(_aligned(c * cls_rows, cls_rows), cls_rows)
            for r in range(A_MID):
                src = pl.ds(_aligned(c * A_SUPER, A_SUPER) + r, cls_rows, stride=A_MID)
                k4_s[r, dst, :] = k_ref[src, :]
                v4_s[r, dst, :] = v_ref[src, :]
            return carry
        lax.fori_loop(0, seq // A_SUPER, fill, 0)

    for r in range(A_MID):
        q4_s[r] = q_ref[pl.ds(r, cls_rows, stride=A_MID), :]

    nb = A_MID
    far_w0, far_variant = _window(st * A_TQ, seq // A_DILATIONS[2], A_TQ, A_TK, A_HALF_WINDOW)
    far_kbase = _aligned(far_w0 * A_MID, A_HALF_WINDOW)

    def geometry(kind, g, u):
        if kind == 0:
            blk = g * nb + u
            w0, variant = _window(st * A_SUPER + blk * A_TQ, seq, A_TQ, A_TK, A_HALF_WINDOW)
            qrows = pl.ds(_aligned(blk * A_TQ, A_TQ), A_TQ)
            krows = pl.ds(_aligned(w0, A_HALF_WINDOW), A_TK)
            return (lambda: q_ref[qrows, :]), krows, k_ref, v_ref, variant, qrows
        if kind == 1:
            w0, variant = _window(st * cls_rows + g * A_TQ, cls_len, A_TQ, A_TK, A_HALF_WINDOW)
            qrows = pl.ds(_aligned(g * A_TQ, A_TQ), A_TQ)
            krows = pl.ds(_aligned(w0, A_HALF_WINDOW), A_TK)
            out_rows = pl.ds(_aligned(u * cls_rows + g * A_TQ, A_TQ), A_TQ)
            return (lambda: q4_s[u, qrows, :]), krows, k4_s.at[u], v4_s.at[u], variant, out_rows
        qrows = pl.ds(g, A_TQ, stride=A_MID)
        krows = pl.ds(far_kbase + g, A_TK, stride=A_MID)
        out_rows = pl.ds(u * cls_rows + g, A_TQ, stride=A_MID)
        return (lambda: q4_s[u, qrows, :]), krows, k4_s.at[u], v4_s.at[u], far_variant, out_rows

    def stage1(group, slot, u):
        kind, g = group
        load_q, krows, kk, _, variant, _ = geometry(kind, g, u)
        p_s[slot, u], mb_s[slot, u] = _scores_head_stacked(load_q(), kk[krows, :], bias_ref[variant])

    def stage2(group, slot, u):
        kind, g = group
        _, krows, _, vv, _, out_rows = geometry(kind, g, u)
        res = _values(p_s[slot, u], mb_s[slot, u], vv[krows, :])
        if kind != A_MERGE_KIND:
            side = A_SIDE_SLOT[kind]
            acc_s[side, out_rows, :], m_s[side, out_rows, :], l_s[side, out_rows, :] = res
            return
        tok = pl.ds(g * A_TQ * A_MID + u, A_TQ, stride=A_MID)
        rows = {A_SIDE_SLOT[0]: tok, A_SIDE_SLOT[2]: out_rows}
        acc, m, l = res
        others = [(acc_s[side, r, :], m_s[side, r, :], l_s[side, r, :]) for side, r in rows.items()]
        m_all = m
        for _, m_o, _ in others:
            m_all = jnp.maximum(m_all, m_o)
        w = jnp.exp2(m - m_all)
        num, den = w * acc, w * l
        for acc_o, m_o, l_o in others:
            w = jnp.exp2(m_o - m_all)
            num, den = num + w * acc_o, den + w * l_o
        out_s[tok, :] = (num / den) * g_ref[tok, :]

    per_kind = A_SUPER // A_TQ // nb
    groups = [(kind, g) for kind in A_KIND_ORDER for g in range(per_kind)]
    _software_pipeline(groups, nb, stage1, stage2)
    o_ref[...] = out_s[...].astype(o_ref.dtype)


def _mixer_a(qa, ka, va, ga, bias):
    b, s, _ = qa.shape
    nst = s // A_SUPER
    ntile = A_WIDTH // LANES
    tile = pl.BlockSpec((None, A_SUPER, LANES), lambda bi, t, st: (bi, st, t))
    full = pl.BlockSpec((None, s, LANES), lambda bi, t, st: (bi, 0, t))
    groups = pltpu.VMEM((len(A_SIDE_SLOT), A_SUPER, LANES), f32)
    return pl.pallas_call(
        functools.partial(_mixer_a_kernel, seq=s),
        grid=(b, ntile, nst),
        in_specs=[tile, full, full, tile,
                  pl.BlockSpec((3, A_TQ, A_TK), lambda bi, t, st: (0, 0, 0))],
        out_specs=tile,
        out_shape=jax.ShapeDtypeStruct((b, s, A_WIDTH), bf16),
        scratch_shapes=[pltpu.VMEM((A_MID, s // A_MID, LANES), f32),
                        pltpu.VMEM((A_MID, s // A_MID, LANES), f32),
                        pltpu.VMEM((A_MID, A_SUPER // A_MID, LANES), f32),
                        pltpu.VMEM((2, A_MID, A_TQ, 2 * A_TK), bf16),
                        pltpu.VMEM((2, A_MID, 2 * A_TQ, LANES), f32),
                        groups, groups, groups,
                        pltpu.VMEM((A_SUPER, LANES), f32)],
        compiler_params=pltpu.CompilerParams(
            dimension_semantics=("arbitrary", "arbitrary", "arbitrary"), vmem_limit_bytes=VMEM_LIMIT),
        name="mixer_a",
    )(qa, ka, va, ga, bias)


def _out_kernel(ya_ref, yb_ref, pa_ref, pb_ref, x_ref, wa_ref, wb_ref, wo_ref, lg_ref, lb_ref, o_ref):
    sub = x_ref.shape[0] // OUT_SUBTILES
    for i in range(OUT_SUBTILES):
        rows = pl.ds(i * sub, sub)
        bra = jnp.dot(ya_ref[rows, :], wa_ref[...], preferred_element_type=f32)
        brb = jnp.dot(yb_ref[rows, :], wb_ref[...], preferred_element_type=f32)
        merged = pa_ref[rows, :].astype(f32) * bra + pb_ref[rows, :].astype(f32) * brb
        out = jnp.dot(merged.astype(bf16), wo_ref[...], preferred_element_type=f32)
        z = DEEPNORM_ALPHA * x_ref[rows, :] + out
        mu = jnp.mean(z, axis=-1, keepdims=True)
        zc = z - mu
        var = jnp.mean(zc * zc, axis=-1, keepdims=True)
        o_ref[rows, :] = zc * lax.rsqrt(var + LN_EPS) * lg_ref[...] + lb_ref[...]


def _finish(ya, yb, pa, pb, x, wa, wb, wo, lg, lb, tm=1024):
    b, s, _ = x.shape
    tok = lambda n: pl.BlockSpec((None, tm, n), lambda bi, i: (bi, i, 0))
    const = lambda r, c: pl.BlockSpec((r, c), lambda bi, i: (0, 0))
    return pl.pallas_call(
        _out_kernel,
        grid=(b, s // tm),
        in_specs=[tok(A_WIDTH), tok(B_WIDTH), tok(D_MODEL), tok(D_MODEL), tok(D_MODEL),
                  const(A_WIDTH, D_MODEL), const(B_WIDTH, D_MODEL), const(D_MODEL, D_MODEL),
                  const(1, D_MODEL), const(1, D_MODEL)],
        out_specs=tok(D_MODEL),
        out_shape=jax.ShapeDtypeStruct((b, s, D_MODEL), f32),
        compiler_params=pltpu.CompilerParams(
            dimension_semantics=("parallel", "parallel"), vmem_limit_bytes=VMEM_LIMIT),
        name="out_proj",
    )(ya, yb, pa, pb, x, wa, wb, wo, lg, lb)


def _encoder_layer(x, w_p, b_gate, sink2, wa, wb, wo, lg, lb, bias_a, bias_b):
    s = x.shape[1]
    assert s % A_SUPER == 0 and s // max(A_DILATIONS) >= A_TK
    tables = _rope_tables(s)
    qa, ka, va, ga, pa, pb, yb = _project_and_mix_b(x, w_p, b_gate, sink2, tables, bias_b)
    ya = _mixer_a(qa, ka, va, ga, bias_a)
    return _finish(ya, yb, pa, pb, x, wa, wb, wo, lg, lb)


def kernel(x_prompt, x_sample, w_in, b_gate, sink_logit, w_branch_a, w_branch_b, w_out, ln_gain, ln_bias):
    assert w_in.shape[0] == DEPTH
    bias_a = jnp.asarray(_band_bias(A_TQ, A_TK, A_HALF_WINDOW))
    bias_b = jnp.asarray(_halo_bias(B_TQ, B_TK, B_HALF_WINDOW))
    y_prompt, y_sample = x_prompt, x_sample
    for l in range(DEPTH):
        w_p = _layout_w_in(w_in[l]).astype(bf16)
        params = (w_p, b_gate[l], sink_logit[l] * LOG2E,
                  w_branch_a[l].astype(bf16), _layout_w_branch_b(w_branch_b[l]).astype(bf16),
                  w_out[l].astype(bf16),
                  ln_gain[l][None, :], ln_bias[l][None, :], bias_a, bias_b)
        y_prompt = _encoder_layer(y_prompt, *params)
        y_sample = _encoder_layer(y_sample, *params)
    return (y_prompt, y_sample)
```

```python
import functools
import math

import numpy as np
import jax
import jax.numpy as jnp
from jax import lax
from jax.experimental import pallas as pl
from jax.experimental.pallas import tpu as pltpu

D_MODEL = 1024
HEAD_DIM = 64
HALF_DIM = HEAD_DIM // 2
A_HEADS = 8
A_WIDTH = A_HEADS * HEAD_DIM
A_DILATIONS = (1, 4, 16)
A_HALF_WINDOW = 64
B_HEADS = 8
B_KV_HEADS = 2
B_GROUP = B_HEADS // B_KV_HEADS
B_WIDTH = B_HEADS * HEAD_DIM
B_KV_WIDTH = B_KV_HEADS * HEAD_DIM
B_HALF_WINDOW = 128
ROPE_THETA = 10000.0
LN_EPS = 1e-5
NEG_INF = -1e30
DEPTH = 1
DEEPNORM_ALPHA = (2.0 * DEPTH) ** 0.25
IN_SPLIT_SIZES = (A_WIDTH, A_WIDTH, A_WIDTH, A_WIDTH, B_WIDTH, B_KV_WIDTH, B_KV_WIDTH, B_WIDTH, D_MODEL, D_MODEL)
IN_OFFSETS = tuple(int(v) for v in np.cumsum((0,) + IN_SPLIT_SIZES[:-1]))

LANES = 128
LOG2E = math.log2(math.e)
QK_SCALE = LOG2E / math.sqrt(HEAD_DIM)

A_TQ = 128
A_TK = A_TQ + 2 * A_HALF_WINDOW
A_SUPER = A_TQ * max(A_DILATIONS)
B_TQ = 128
B_TK = B_TQ + 2 * B_HALF_WINDOW
A_MID = A_DILATIONS[1]
assert A_DILATIONS == (1, A_MID, A_MID * A_MID)
OUT_SUBTILES = 4
A_BLOCKS_PER_GROUP = 1
A_KIND_ORDER = (0, 2, 1)
A_MERGE_KIND = A_KIND_ORDER[-1]
A_SIDE_SLOT = {0: 0, 2: 1}

VMEM_LIMIT = 56 * 1024 * 1024

f32 = jnp.float32
bf16 = jnp.bfloat16


B_PAIRS = B_HEADS // 2
assert B_KV_HEADS == 2


def _pair_tiles(w):
    d, t = w.shape[0], w.shape[1]
    return w.reshape(d, t, 2, 2, HALF_DIM).transpose(0, 1, 3, 2, 4).reshape(d, t * LANES)


def _layout_w_in(w_in):
    d = w_in.shape[0]
    seg = [w_in[:, o:o + n] for o, n in zip(IN_OFFSETS, IN_SPLIT_SIZES)]
    qa = _pair_tiles(seg[0].reshape(d, A_HEADS // 2, 2, HEAD_DIM))
    ka = _pair_tiles(seg[1].reshape(d, A_HEADS // 2, 2, HEAD_DIM))
    by_tile = lambda w: w.reshape(d, B_KV_HEADS, B_PAIRS, HEAD_DIM).transpose(0, 2, 1, 3)
    qb = _pair_tiles(by_tile(seg[4]))
    kb = _pair_tiles(seg[5].reshape(d, 1, B_KV_HEADS, HEAD_DIM))
    gb = by_tile(seg[7]).reshape(d, B_WIDTH)
    return jnp.concatenate([qa, ka, seg[2], seg[3], qb, kb, seg[6], gb, seg[8], seg[9]], axis=1)


def _layout_w_branch_b(w):
    return w.reshape(B_KV_HEADS, B_PAIRS, HEAD_DIM, -1).transpose(1, 0, 2, 3).reshape(w.shape)


P_QA, P_KA, P_VA, P_GA = 0, 512, 1024, 1536
P_QB, P_KB, P_GB = 2048, 2560, 2816
P_PA, P_PB = 3328, 4352
P_TOTAL = 5376
assert P_TOTAL == sum(IN_SPLIT_SIZES)


def _rope_tables(seq):
    inv_freq = ROPE_THETA ** (-jnp.arange(HALF_DIM, dtype=f32) / HALF_DIM)
    ang = jnp.arange(seq, dtype=f32)[:, None] * inv_freq[None, :]
    cos = jnp.tile(jnp.cos(ang), (1, 4))
    sin = jnp.tile(jnp.sin(ang), (1, 4))
    sign = jnp.where(jnp.arange(LANES) < 2 * HALF_DIM, -1.0, 1.0).astype(f32)
    sin = sin * sign[None, :]
    return cos * QK_SCALE, sin * QK_SCALE, cos, sin


def _band_bias(tq, tk, half_window):
    i = np.arange(tq)[:, None]
    j = np.arange(tk)[None, :]
    out = []
    for shift in (0, half_window, tk - tq):
        ok = np.abs(j - shift - i) <= half_window
        out.append(np.where(ok, 0.0, NEG_INF).astype(np.float32))
    return np.stack(out, 0)


def _halo_bias(tq, tk, half_window):
    i = np.arange(tq)[:, None]
    j = np.arange(tk)[None, :]
    band = np.abs(j - half_window - i) <= half_window
    out = [band & (j >= half_window), band, band & (j < tk - half_window)]
    return np.stack([np.where(ok, 0.0, NEG_INF).astype(np.float32) for ok in out], 0)


def _rope(h, c, s):
    tiles = []
    for t in range(h.shape[1] // LANES):
        ht = h[:, t * LANES:(t + 1) * LANES]
        tiles.append(ht * c + pltpu.roll(ht, 2 * HALF_DIM, 1) * s)
    return jnp.concatenate(tiles, axis=1)


def _proj_kernel(sink_ref, x_ref, xl_ref, xr_ref, w_ref, bg_ref, cq_ref, sq_ref, ck_ref, sk_ref,
                 ckl_ref, skl_ref, ckr_ref, skr_ref, bias_ref,
                 qa_o, ka_o, va_o, ga_o, pa_o, pb_o, yb_o,
                 qb_s, gb_s, kb_s, vb_s, p_s, mb_s):
    i, nt = pl.program_id(1), pl.num_programs(1)
    tm = x_ref.shape[0]
    xb = x_ref[...].astype(bf16)
    cq, sq, ck, sk = cq_ref[...], sq_ref[...], ck_ref[...], sk_ref[...]

    def proj(c0, n, lhs=xb):
        return jnp.dot(lhs, w_ref[:, c0:c0 + n], preferred_element_type=f32)

    x_ext = jnp.concatenate([xl_ref[...].astype(bf16), xb, xr_ref[...].astype(bf16)], axis=0)
    kvb = proj(P_KB, 2 * B_KV_WIDTH, x_ext)
    kb_s[...] = _rope(kvb[:, :B_KV_WIDTH], jnp.concatenate([ckl_ref[...], ck, ckr_ref[...]], axis=0),
                      jnp.concatenate([skl_ref[...], sk, skr_ref[...]], axis=0))
    vb_s[...] = kvb[:, B_KV_WIDTH:]
    qb_s[...] = _rope(proj(P_QB, B_WIDTH), cq, sq)
    gb_s[...] = jax.nn.silu(proj(P_GB, B_WIDTH))

    nb = tm // B_TQ
    first = lax.broadcasted_iota(jnp.int32, (B_TQ, LANES), 1) < HEAD_DIM

    def stage1(blk, slot, _):
        t, u = divmod(blk, nb)
        rows, cols = pl.ds(u * B_TQ, B_TQ), pl.ds(t * LANES, LANES)
        variant = jnp.where(i == 0, 0, 1) if u == 0 else jnp.where(i == nt - 1, 2, 1) if u == nb - 1 else 1
        sink0 = jnp.full((B_TQ, LANES), sink_ref[t], f32)
        sink1 = jnp.full((B_TQ, LANES), sink_ref[t + B_GROUP], f32)
        p_s[slot], mb_s[slot] = _scores_head_blocked(qb_s[rows, cols], kb_s[pl.ds(u * B_TQ, B_TK), :],
                                                    bias_ref[variant], sink0, sink1)

    def stage2(blk, slot, _):
        t, u = divmod(blk, nb)
        rows, cols = pl.ds(u * B_TQ, B_TQ), pl.ds(t * LANES, LANES)
        acc, m, l = _values(p_s[slot], mb_s[slot], vb_s[pl.ds(u * B_TQ, B_TK), :])
        l = l + jnp.exp2(jnp.where(first, sink_ref[t], sink_ref[t + B_GROUP]) - m)
        yb_o[rows, cols] = ((acc / l) * gb_s[rows, cols]).astype(yb_o.dtype)

    def seg_pa():
        pa_o[...] = jax.nn.sigmoid(proj(P_PA, D_MODEL) + bg_ref[0:1, :]).astype(pa_o.dtype)

    def seg_pb():
        pb_o[...] = jax.nn.sigmoid(proj(P_PB, D_MODEL) + bg_ref[1:2, :]).astype(pb_o.dtype)

    def seg_ga():
        ga_o[...] = jax.nn.silu(proj(P_GA, A_WIDTH))

    def seg_qa():
        qa_o[...] = _rope(proj(P_QA, A_WIDTH), cq, sq)

    def seg_ka():
        ka_o[...] = _rope(proj(P_KA, A_WIDTH), ck, sk)

    segments = (seg_pa, seg_pb, seg_ga, seg_qa, seg_ka)
    n_blocks = B_PAIRS * nb
    between = [None] * (n_blocks + 1)
    for j, seg in enumerate(segments):
        between[j * n_blocks // len(segments)] = seg
    _software_pipeline(list(range(n_blocks)), 1, stage1, stage2, between=between)
    va_o[...] = proj(P_VA, A_WIDTH)


def _project_and_mix_b(x, w_p, b_gate, sink2, tables, bias_b, tm=512):
    b, s, _ = x.shape
    halo = B_HALF_WINDOW
    per_tile, last = tm // halo, s // halo - 1
    tok = lambda n: pl.BlockSpec((None, tm, n), lambda bi, i, sk: (bi, i, 0))
    left = lambda i: jnp.maximum(i * per_tile - 1, 0)
    right = lambda i: jnp.minimum((i + 1) * per_tile, last)
    tab = pl.BlockSpec((tm, LANES), lambda bi, i, sk: (i, 0))
    tab_l = pl.BlockSpec((halo, LANES), lambda bi, i, sk: (left(i), 0))
    tab_r = pl.BlockSpec((halo, LANES), lambda bi, i, sk: (right(i), 0))
    cq, sq, ck, sk_ = tables
    widths = (A_WIDTH, A_WIDTH, A_WIDTH, A_WIDTH, D_MODEL, D_MODEL, B_WIDTH)
    dtypes = (f32, f32, f32, f32, bf16, bf16, bf16)
    return pl.pallas_call(
        _proj_kernel,
        grid_spec=pltpu.PrefetchScalarGridSpec(
            num_scalar_prefetch=1,
            grid=(b, s // tm),
            in_specs=[tok(D_MODEL),
                      pl.BlockSpec((None, halo, D_MODEL), lambda bi, i, sk: (bi, left(i), 0)),
                      pl.BlockSpec((None, halo, D_MODEL), lambda bi, i, sk: (bi, right(i), 0)),
                      pl.BlockSpec((D_MODEL, P_TOTAL), lambda bi, i, sk: (0, 0)),
                      pl.BlockSpec((2, D_MODEL), lambda bi, i, sk: (0, 0)),
                      tab, tab, tab, tab, tab_l, tab_l, tab_r, tab_r,
                      pl.BlockSpec((3, B_TQ, B_TK), lambda bi, i, sk: (0, 0, 0))],
            out_specs=[tok(n) for n in widths],
            scratch_shapes=[pltpu.VMEM((tm, B_WIDTH), f32), pltpu.VMEM((tm, B_WIDTH), f32),
                            pltpu.VMEM((tm + 2 * halo, B_KV_WIDTH), f32), pltpu.VMEM((tm + 2 * halo, B_KV_WIDTH), f32),
                            pltpu.VMEM((2, B_TQ, 2 * B_TK), bf16), pltpu.VMEM((2, 2 * B_TQ, LANES), f32)],
        ),
        out_shape=[jax.ShapeDtypeStruct((b, s, n), dt) for n, dt in zip(widths, dtypes)],
        compiler_params=pltpu.CompilerParams(
            dimension_semantics=("parallel", "parallel"), vmem_limit_bytes=VMEM_LIMIT),
        name="in_proj_mixer_b",
    )(sink2, x, x, x, w_p, b_gate, cq, sq, ck, sk_, ck, sk_, ck, sk_, bias_b)


def _pair_slot0():
    lane = lax.broadcasted_iota(jnp.int32, (1, LANES), 1)
    return (lane % HEAD_DIM) < HALF_DIM


def _exp_rows(s, mb):
    return [jnp.exp2(s[:, c * LANES:(c + 1) * LANES] - mb).astype(bf16) for c in range(s.shape[1] // LANES)]


def _row_max(s):
    return jnp.broadcast_to(jnp.max(s, axis=-1, keepdims=True), (s.shape[0], LANES))


def _scores_head_stacked(qf, kf, bias):
    tq = qf.shape[0]
    slot0 = _pair_slot0()
    qb = qf.astype(bf16)
    zero = jnp.zeros_like(qb)
    q2 = jnp.concatenate([jnp.where(slot0, qb, zero), jnp.where(slot0, zero, qb)], axis=0)
    s = lax.dot_general(q2, kf.astype(bf16), (((1,), (1,)), ((), ())), preferred_element_type=f32)
    s0, s1 = s[:tq] + bias, s[tq:] + bias
    m0, m1 = _row_max(s0), _row_max(s1)
    return jnp.concatenate(_exp_rows(s0, m0) + _exp_rows(s1, m1), axis=1), jnp.concatenate([m0, m1], axis=0)


def _scores_head_blocked(qf, kf, bias, sink0, sink1):
    tk = kf.shape[0]
    slot0 = _pair_slot0()
    kb = kf.astype(bf16)
    zero = jnp.zeros_like(kb)
    k2 = jnp.concatenate([jnp.where(slot0, kb, zero), jnp.where(slot0, zero, kb)], axis=0)
    s = lax.dot_general(qf.astype(bf16), k2, (((1,), (1,)), ((), ())), preferred_element_type=f32)
    s0, s1 = s[:, :tk] + bias, s[:, tk:] + bias
    m0, m1 = jnp.maximum(_row_max(s0), sink0), jnp.maximum(_row_max(s1), sink1)
    return jnp.concatenate(_exp_rows(s0, m0) + _exp_rows(s1, m1), axis=1), jnp.concatenate([m0, m1], axis=0)


def _values(p, mb, vf):
    tq = p.shape[0]
    key_lane = lax.broadcasted_iota(jnp.int32, vf.shape, 1)
    sel0 = jnp.where(key_lane < HEAD_DIM, 1.0, 0.0).astype(bf16)
    sel1 = jnp.where(key_lane < HEAD_DIM, 0.0, 1.0).astype(bf16)
    vb = vf.astype(bf16)
    w = jnp.concatenate([jnp.concatenate([vb * sel0, sel0], axis=1),
                         jnp.concatenate([vb * sel1, sel1], axis=1)], axis=0)
    o = jnp.dot(p, w, preferred_element_type=f32)
    first = lax.broadcasted_iota(jnp.int32, (tq, LANES), 1) < HEAD_DIM
    return o[:, :LANES], jnp.where(first, mb[:tq], mb[tq:]), o[:, LANES:]


def _aligned(x, multiple):
    if isinstance(x, int):
        assert x % multiple == 0
        return x
    return pl.multiple_of(x, multiple)


def _window(q0, length, tq, tk, half_window):
    w0 = jnp.clip(q0 - half_window, 0, length - tk)
    variant = jnp.where(q0 == 0, 0, jnp.where(q0 == length - tq, 2, 1))
    return w0, variant


def _software_pipeline(groups, nb, stage1, stage2, between=()):
    extra = list(between) + [None] * (len(groups) + 1 - len(between))
    for i in range(len(groups) + 1):
        if extra[i] is not None:
            extra[i]()
        if i < len(groups):
            for u in range(nb):
                stage1(groups[i], i % 2, u)
        if i > 0:
            for u in range(nb):
                stage2(groups[i - 1], (i - 1) % 2, u)


def _mixer_a_kernel(q_ref, k_ref, v_ref, g_ref, bias_ref, o_ref,
                    k4_s, v4_s, q4_s, p_s, mb_s, acc_s, m_s, l_s, out_s, *, seq):
    st = pl.program_id(2)
    cls_len = seq // A_MID
    cls_rows = A_SUPER // A_MID

    @pl.when(st == 0)
    def _():
        def fill(c, carry):
            dst = pl.ds(_aligned(c * cls_rows, cls_rows), cls_rows)
            for r in range(A_MID):
                src = pl.ds(_aligned(c * A_SUPER, A_SUPER) + r, cls_rows, stride=A_MID)
                k4_s[r, dst, :] = k_ref[src, :]
                v4_s[r, dst, :] = v_ref[src, :]
            return carry
        lax.fori_loop(0, seq // A_SUPER, fill, 0)

    for r in range(A_MID):
        q4_s[r] = q_ref[pl.ds(r, cls_rows, stride=A_MID), :]

    nb = A_BLOCKS_PER_GROUP
    far_w0, far_variant = _window(st * A_TQ, seq // A_DILATIONS[2], A_TQ, A_TK, A_HALF_WINDOW)
    far_kbase = _aligned(far_w0 * A_MID, A_HALF_WINDOW)

    def geometry(kind, idx):
        if kind == 0:
            blk = idx
            w0, variant = _window(st * A_SUPER + blk * A_TQ, seq, A_TQ, A_TK, A_HALF_WINDOW)
            qrows = pl.ds(_aligned(blk * A_TQ, A_TQ), A_TQ)
            krows = pl.ds(_aligned(w0, A_HALF_WINDOW), A_TK)
            return (lambda: q_ref[qrows, :]), krows, k_ref, v_ref, variant, qrows
        g, u = divmod(idx, A_MID)
        if kind == 1:
            w0, variant = _window(st * cls_rows + g * A_TQ, cls_len, A_TQ, A_TK, A_HALF_WINDOW)
            qrows = pl.ds(_aligned(g * A_TQ, A_TQ), A_TQ)
            krows = pl.ds(_aligned(w0, A_HALF_WINDOW), A_TK)
            out_rows = pl.ds(_aligned(u * cls_rows + g * A_TQ, A_TQ), A_TQ)
            return (lambda: q4_s[u, qrows, :]), krows, k4_s.at[u], v4_s.at[u], variant, out_rows
        qrows = pl.ds(g, A_TQ, stride=A_MID)
        krows = pl.ds(far_kbase + g, A_TK, stride=A_MID)
        out_rows = pl.ds(u * cls_rows + g, A_TQ, stride=A_MID)
        return (lambda: q4_s[u, qrows, :]), krows, k4_s.at[u], v4_s.at[u], far_variant, out_rows

    def stage1(group, slot, u):
        kind, first_block = group
        load_q, krows, kk, _, variant, _ = geometry(kind, first_block + u)
        p_s[slot, u], mb_s[slot, u] = _scores_head_stacked(load_q(), kk[krows, :], bias_ref[variant])

    def stage2(group, slot, u):
        kind, first_block = group
        _, krows, _, vv, _, out_rows = geometry(kind, first_block + u)
        res = _values(p_s[slot, u], mb_s[slot, u], vv[krows, :])
        if kind != A_MERGE_KIND:
            side = A_SIDE_SLOT[kind]
            acc_s[side, out_rows, :], m_s[side, out_rows, :], l_s[side, out_rows, :] = res
            return
        g, cls = divmod(first_block + u, A_MID)
        tok = pl.ds(g * A_TQ * A_MID + cls, A_TQ, stride=A_MID)
        rows = {A_SIDE_SLOT[0]: tok, A_SIDE_SLOT[2]: out_rows}
        acc, m, l = res
        others = [(acc_s[side, r, :], m_s[side, r, :], l_s[side, r, :]) for side, r in rows.items()]
        m_all = m
        for _, m_o, _ in others:
            m_all = jnp.maximum(m_all, m_o)
        w = jnp.exp2(m - m_all)
        num, den = w * acc, w * l
        for acc_o, m_o, l_o in others:
            w = jnp.exp2(m_o - m_all)
            num, den = num + w * acc_o, den + w * l_o
        out_s[tok, :] = (num / den) * g_ref[tok, :]

    groups = [(kind, blk) for kind in A_KIND_ORDER for blk in range(0, A_SUPER // A_TQ, nb)]
    _software_pipeline(groups, nb, stage1, stage2)
    o_ref[...] = out_s[...].astype(o_ref.dtype)


def _mixer_a(qa, ka, va, ga, bias):
    b, s, _ = qa.shape
    nst = s // A_SUPER
    ntile = A_WIDTH // LANES
    tile = pl.BlockSpec((None, A_SUPER, LANES), lambda bi, t, st: (bi, st, t))
    full = pl.BlockSpec((None, s, LANES), lambda bi, t, st: (bi, 0, t))
    groups = pltpu.VMEM((len(A_SIDE_SLOT), A_SUPER, LANES), f32)
    return pl.pallas_call(
        functools.partial(_mixer_a_kernel, seq=s),
        grid=(b, ntile, nst),
        in_specs=[tile, full, full, tile,
                  pl.BlockSpec((3, A_TQ, A_TK), lambda bi, t, st: (0, 0, 0))],
        out_specs=tile,
        out_shape=jax.ShapeDtypeStruct((b, s, A_WIDTH), bf16),
        scratch_shapes=[pltpu.VMEM((A_MID, s // A_MID, LANES), f32),
                        pltpu.VMEM((A_MID, s // A_MID, LANES), f32),
                        pltpu.VMEM((A_MID, A_SUPER // A_MID, LANES), f32),
                        pltpu.VMEM((2, A_BLOCKS_PER_GROUP, A_TQ, 2 * A_TK), bf16),
                        pltpu.VMEM((2, A_BLOCKS_PER_GROUP, 2 * A_TQ, LANES), f32),
                        groups, groups, groups,
                        pltpu.VMEM((A_SUPER, LANES), f32)],
        compiler_params=pltpu.CompilerParams(
            dimension_semantics=("arbitrary", "arbitrary", "arbitrary"), vmem_limit_bytes=VMEM_LIMIT),
        name="mixer_a",
    )(qa, ka, va, ga, bias)


def _out_kernel(ya_ref, yb_ref, pa_ref, pb_ref, x_ref, wa_ref, wb_ref, wo_ref, lg_ref, lb_ref, o_ref):
    sub = x_ref.shape[0] // OUT_SUBTILES
    for i in range(OUT_SUBTILES):
        rows = pl.ds(i * sub, sub)
        bra = jnp.dot(ya_ref[rows, :], wa_ref[...], preferred_element_type=f32)
        brb = jnp.dot(yb_ref[rows, :], wb_ref[...], preferred_element_type=f32)
        merged = pa_ref[rows, :].astype(f32) * bra + pb_ref[rows, :].astype(f32) * brb
        out = jnp.dot(merged.astype(bf16), wo_ref[...], preferred_element_type=f32)
        z = DEEPNORM_ALPHA * x_ref[rows, :] + out
        mu = jnp.mean(z, axis=-1, keepdims=True)
        zc = z - mu
        var = jnp.mean(zc * zc, axis=-1, keepdims=True)
        o_ref[rows, :] = zc * lax.rsqrt(var + LN_EPS) * lg_ref[...] + lb_ref[...]


def _finish(ya, yb, pa, pb, x, wa, wb, wo, lg, lb, tm=1024):
    b, s, _ = x.shape
    tok = lambda n: pl.BlockSpec((None, tm, n), lambda bi, i: (bi, i, 0))
    const = lambda r, c: pl.BlockSpec((r, c), lambda bi, i: (0, 0))
    return pl.pallas_call(
        _out_kernel,
        grid=(b, s // tm),
        in_specs=[tok(A_WIDTH), tok(B_WIDTH), tok(D_MODEL), tok(D_MODEL), tok(D_MODEL),
                  const(A_WIDTH, D_MODEL), const(B_WIDTH, D_MODEL), const(D_MODEL, D_MODEL),
                  const(1, D_MODEL), const(1, D_MODEL)],
        out_specs=tok(D_MODEL),
        out_shape=jax.ShapeDtypeStruct((b, s, D_MODEL), f32),
        compiler_params=pltpu.CompilerParams(
            dimension_semantics=("parallel", "parallel"), vmem_limit_bytes=VMEM_LIMIT),
        name="out_proj",
    )(ya, yb, pa, pb, x, wa, wb, wo, lg, lb)


def _encoder_layer(x, w_p, b_gate, sink2, wa, wb, wo, lg, lb, bias_a, bias_b):
    s = x.shape[1]
    assert s % A_SUPER == 0 and s // max(A_DILATIONS) >= A_TK
    tables = _rope_tables(s)
    qa, ka, va, ga, pa, pb, yb = _project_and_mix_b(x, w_p, b_gate, sink2, tables, bias_b)
    ya = _mixer_a(qa, ka, va, ga, bias_a)
    return _finish(ya, yb, pa, pb, x, wa, wb, wo, lg, lb)


def kernel(x_prompt, x_sample, w_in, b_gate, sink_logit, w_branch_a, w_branch_b, w_out, ln_gain, ln_bias):
    assert w_in.shape[0] == DEPTH
    bias_a = jnp.asarray(_band_bias(A_TQ, A_TK, A_HALF_WINDOW))
    bias_b = jnp.asarray(_halo_bias(B_TQ, B_TK, B_HALF_WINDOW))
    y_prompt, y_sample = x_prompt, x_sample
    for l in range(DEPTH):
        w_p = _layout_w_in(w_in[l]).astype(bf16)
        params = (w_p, b_gate[l], sink_logit[l] * LOG2E,
                  w_branch_a[l].astype(bf16), _layout_w_branch_b(w_branch_b[l]).astype(bf16),
                  w_out[l].astype(bf16),
                  ln_gain[l][None, :], ln_bias[l][None, :], bias_a, bias_b)
        y_prompt = _encoder_layer(y_prompt, *params)
        y_sample = _encoder_layer(y_sample, *params)
    return (y_prompt, y_sample)
```

```python
import functools
import math

import numpy as np
import jax
import jax.numpy as jnp
from jax import lax
from jax.experimental import pallas as pl
from jax.experimental.pallas import tpu as pltpu

D_MODEL = 1024
HEAD_DIM = 64
HALF_DIM = HEAD_DIM // 2
A_HEADS = 8
A_WIDTH = A_HEADS * HEAD_DIM
A_DILATIONS = (1, 4, 16)
A_HALF_WINDOW = 64
B_HEADS = 8
B_KV_HEADS = 2
B_GROUP = B_HEADS // B_KV_HEADS
B_WIDTH = B_HEADS * HEAD_DIM
B_KV_WIDTH = B_KV_HEADS * HEAD_DIM
B_HALF_WINDOW = 128
ROPE_THETA = 10000.0
LN_EPS = 1e-5
NEG_INF = -1e30
DEPTH = 1
DEEPNORM_ALPHA = (2.0 * DEPTH) ** 0.25
IN_SPLIT_SIZES = (A_WIDTH, A_WIDTH, A_WIDTH, A_WIDTH, B_WIDTH, B_KV_WIDTH, B_KV_WIDTH, B_WIDTH, D_MODEL, D_MODEL)
IN_OFFSETS = tuple(int(v) for v in np.cumsum((0,) + IN_SPLIT_SIZES[:-1]))

LANES = 128
LOG2E = math.log2(math.e)
QK_SCALE = LOG2E / math.sqrt(HEAD_DIM)

A_TQ = 128
A_TK = A_TQ + 2 * A_HALF_WINDOW
A_SUPER = A_TQ * max(A_DILATIONS)
B_TQ = 128
B_TK = B_TQ + 2 * B_HALF_WINDOW
A_MID = A_DILATIONS[1]
assert A_DILATIONS == (1, A_MID, A_MID * A_MID)
OUT_SUBTILES = 4
A_BLOCKS_PER_GROUP = 1
A_KIND_ORDER = (0, 2, 1)
A_MERGE_KIND = A_KIND_ORDER[-1]
A_SIDE_SLOT = {0: 0, 2: 1}

VMEM_LIMIT = 56 * 1024 * 1024

f32 = jnp.float32
bf16 = jnp.bfloat16


B_PAIRS = B_HEADS // 2
assert B_KV_HEADS == 2


def _pair_tiles(w):
    d, t = w.shape[0], w.shape[1]
    return w.reshape(d, t, 2, 2, HALF_DIM).transpose(0, 1, 3, 2, 4).reshape(d, t * LANES)


def _layout_w_in(w_in):
    d = w_in.shape[0]
    seg = [w_in[:, o:o + n] for o, n in zip(IN_OFFSETS, IN_SPLIT_SIZES)]
    qa = _pair_tiles(seg[0].reshape(d, A_HEADS // 2, 2, HEAD_DIM))
    ka = _pair_tiles(seg[1].reshape(d, A_HEADS // 2, 2, HEAD_DIM))
    by_tile = lambda w: w.reshape(d, B_KV_HEADS, B_PAIRS, HEAD_DIM).transpose(0, 2, 1, 3)
    qb = _pair_tiles(by_tile(seg[4]))
    kb = _pair_tiles(seg[5].reshape(d, 1, B_KV_HEADS, HEAD_DIM))
    gb = by_tile(seg[7]).reshape(d, B_WIDTH)
    return jnp.concatenate([qa, ka, seg[2], seg[3], qb, kb, seg[6], gb, seg[8], seg[9]], axis=1)


def _layout_w_branch_b(w):
    return w.reshape(B_KV_HEADS, B_PAIRS, HEAD_DIM, -1).transpose(1, 0, 2, 3).reshape(w.shape)


P_QA, P_KA, P_VA, P_GA = 0, 512, 1024, 1536
P_QB, P_KB, P_GB = 2048, 2560, 2816
P_PA, P_PB = 3328, 4352
P_TOTAL = 5376
assert P_TOTAL == sum(IN_SPLIT_SIZES)


def _rope_tables(seq):
    inv_freq = ROPE_THETA ** (-jnp.arange(HALF_DIM, dtype=f32) / HALF_DIM)
    ang = jnp.arange(seq, dtype=f32)[:, None] * inv_freq[None, :]
    cos = jnp.tile(jnp.cos(ang), (1, 4))
    sin = jnp.tile(jnp.sin(ang), (1, 4))
    sign = jnp.where(jnp.arange(LANES) < 2 * HALF_DIM, -1.0, 1.0).astype(f32)
    sin = sin * sign[None, :]
    return cos * QK_SCALE, sin * QK_SCALE, cos, sin


def _band_bias(tq, tk, half_window):
    i = np.arange(tq)[:, None]
    j = np.arange(tk)[None, :]
    out = []
    for shift in (0, half_window, tk - tq):
        ok = np.abs(j - shift - i) <= half_window
        out.append(np.where(ok, 0.0, NEG_INF).astype(np.float32))
    return np.stack(out, 0)


def _halo_bias(tq, tk, half_window):
    i = np.arange(tq)[:, None]
    j = np.arange(tk)[None, :]
    band = np.abs(j - half_window - i) <= half_window
    out = [band & (j >= half_window), band, band & (j < tk - half_window)]
    return np.stack([np.where(ok, 0.0, NEG_INF).astype(np.float32) for ok in out], 0)


def _rope(h, c, s):
    tiles = []
    for t in range(h.shape[1] // LANES):
        ht = h[:, t * LANES:(t + 1) * LANES]
        tiles.append(ht * c + pltpu.roll(ht, 2 * HALF_DIM, 1) * s)
    return jnp.concatenate(tiles, axis=1)


def _proj_kernel(sink_ref, x_ref, xl_ref, xr_ref, w_ref, bg_ref, cq_ref, sq_ref, ck_ref, sk_ref,
                 ckl_ref, skl_ref, ckr_ref, skr_ref, bias_ref,
                 qa_o, ka_o, va_o, ga_o, pa_o, pb_o, yb_o,
                 qb_s, gb_s, kb_s, vb_s, p_s, mb_s):
    i, nt = pl.program_id(1), pl.num_programs(1)
    tm = x_ref.shape[0]
    xb = x_ref[...].astype(bf16)
    cq, sq, ck, sk = cq_ref[...], sq_ref[...], ck_ref[...], sk_ref[...]

    def proj(c0, n, lhs=xb):
        return jnp.dot(lhs, w_ref[:, c0:c0 + n], preferred_element_type=f32)

    x_ext = jnp.concatenate([xl_ref[...].astype(bf16), xb, xr_ref[...].astype(bf16)], axis=0)
    kvb = proj(P_KB, 2 * B_KV_WIDTH, x_ext)
    kb_s[...] = _rope(kvb[:, :B_KV_WIDTH], jnp.concatenate([ckl_ref[...], ck, ckr_ref[...]], axis=0),
                      jnp.concatenate([skl_ref[...], sk, skr_ref[...]], axis=0))
    vb_s[...] = kvb[:, B_KV_WIDTH:]
    qb_s[...] = _rope(proj(P_QB, B_WIDTH), cq, sq)
    gb_s[...] = jax.nn.silu(proj(P_GB, B_WIDTH))

    nb = tm // B_TQ
    first = lax.broadcasted_iota(jnp.int32, (B_TQ, LANES), 1) < HEAD_DIM

    def stage1(blk, slot, _):
        t, u = divmod(blk, nb)
        rows, cols = pl.ds(u * B_TQ, B_TQ), pl.ds(t * LANES, LANES)
        variant = jnp.where(i == 0, 0, 1) if u == 0 else jnp.where(i == nt - 1, 2, 1) if u == nb - 1 else 1
        sink0 = jnp.full((B_TQ, LANES), sink_ref[t], f32)
        sink1 = jnp.full((B_TQ, LANES), sink_ref[t + B_GROUP], f32)
        p_s[slot], mb_s[slot] = _scores_head_blocked(qb_s[rows, cols], kb_s[pl.ds(u * B_TQ, B_TK), :],
                                                    bias_ref[variant], sink0, sink1)

    def stage2(blk, slot, _):
        t, u = divmod(blk, nb)
        rows, cols = pl.ds(u * B_TQ, B_TQ), pl.ds(t * LANES, LANES)
        acc, m, l = _values(p_s[slot], mb_s[slot], vb_s[pl.ds(u * B_TQ, B_TK), :])
        l = l + jnp.exp2(jnp.where(first, sink_ref[t], sink_ref[t + B_GROUP]) - m)
        yb_o[rows, cols] = ((acc / l) * gb_s[rows, cols]).astype(yb_o.dtype)

    def seg_pa():
        pa_o[...] = jax.nn.sigmoid(proj(P_PA, D_MODEL) + bg_ref[0:1, :]).astype(pa_o.dtype)

    def seg_pb():
        pb_o[...] = jax.nn.sigmoid(proj(P_PB, D_MODEL) + bg_ref[1:2, :]).astype(pb_o.dtype)

    def seg_ga():
        ga_o[...] = jax.nn.silu(proj(P_GA, A_WIDTH))

    def seg_qa():
        qa_o[...] = _rope(proj(P_QA, A_WIDTH), cq, sq)

    def seg_ka():
        ka_o[...] = _rope(proj(P_KA, A_WIDTH), ck, sk)

    segments = (seg_pa, seg_pb, seg_ga, seg_qa, seg_ka)
    n_blocks = B_PAIRS * nb
    between = [None] * (n_blocks + 1)
    for j, seg in enumerate(segments):
        between[j * n_blocks // len(segments)] = seg
    _software_pipeline(list(range(n_blocks)), 1, stage1, stage2, between=between)
    va_o[...] = proj(P_VA, A_WIDTH)


def _project_and_mix_b(x, w_p, b_gate, sink2, tables, bias_b, tm=512):
    b, s, _ = x.shape
    halo = B_HALF_WINDOW
    per_tile, last = tm // halo, s // halo - 1
    tok = lambda n: pl.BlockSpec((None, tm, n), lambda bi, i, sk: (bi, i, 0))
    left = lambda i: jnp.maximum(i * per_tile - 1, 0)
    right = lambda i: jnp.minimum((i + 1) * per_tile, last)
    tab = pl.BlockSpec((tm, LANES), lambda bi, i, sk: (i, 0))
    tab_l = pl.BlockSpec((halo, LANES), lambda bi, i, sk: (left(i), 0))
    tab_r = pl.BlockSpec((halo, LANES), lambda bi, i, sk: (right(i), 0))
    cq, sq, ck, sk_ = tables
    widths = (A_WIDTH, A_WIDTH, A_WIDTH, A_WIDTH, D_MODEL, D_MODEL, B_WIDTH)
    dtypes = (f32, f32, f32, f32, bf16, bf16, bf16)
    return pl.pallas_call(
        _proj_kernel,
        grid_spec=pltpu.PrefetchScalarGridSpec(
            num_scalar_prefetch=1,
            grid=(b, s // tm),
            in_specs=[tok(D_MODEL),
                      pl.BlockSpec((None, halo, D_MODEL), lambda bi, i, sk: (bi, left(i), 0)),
                      pl.BlockSpec((None, halo, D_MODEL), lambda bi, i, sk: (bi, right(i), 0)),
                      pl.BlockSpec((D_MODEL, P_TOTAL), lambda bi, i, sk: (0, 0)),
                      pl.BlockSpec((2, D_MODEL), lambda bi, i, sk: (0, 0)),
                      tab, tab, tab, tab, tab_l, tab_l, tab_r, tab_r,
                      pl.BlockSpec((3, B_TQ, B_TK), lambda bi, i, sk: (0, 0, 0))],
            out_specs=[tok(n) for n in widths],
            scratch_shapes=[pltpu.VMEM((tm, B_WIDTH), f32), pltpu.VMEM((tm, B_WIDTH), f32),
                            pltpu.VMEM((tm + 2 * halo, B_KV_WIDTH), f32), pltpu.VMEM((tm + 2 * halo, B_KV_WIDTH), f32),
                            pltpu.VMEM((2, B_TQ, 2 * B_TK), bf16), pltpu.VMEM((2, 2 * B_TQ, LANES), f32)],
        ),
        out_shape=[jax.ShapeDtypeStruct((b, s, n), dt) for n, dt in zip(widths, dtypes)],
        compiler_params=pltpu.CompilerParams(
            dimension_semantics=("parallel", "parallel"), vmem_limit_bytes=VMEM_LIMIT),
        name="in_proj_mixer_b",
    )(sink2, x, x, x, w_p, b_gate, cq, sq, ck, sk_, ck, sk_, ck, sk_, bias_b)


def _pair_slot0():
    lane = lax.broadcasted_iota(jnp.int32, (1, LANES), 1)
    return (lane % HEAD_DIM) < HALF_DIM


def _exp_rows(s, mb):
    return [jnp.exp2(s[:, c * LANES:(c + 1) * LANES] - mb).astype(bf16) for c in range(s.shape[1] // LANES)]


def _row_max(s):
    return jnp.broadcast_to(jnp.max(s, axis=-1, keepdims=True), (s.shape[0], LANES))


def _scores_head_stacked(qf, kf, bias):
    tq = qf.shape[0]
    slot0 = _pair_slot0()
    qb = qf.astype(bf16)
    zero = jnp.zeros_like(qb)
    q2 = jnp.concatenate([jnp.where(slot0, qb, zero), jnp.where(slot0, zero, qb)], axis=0)
    s = lax.dot_general(q2, kf.astype(bf16), (((1,), (1,)), ((), ())), preferred_element_type=f32)
    s0, s1 = s[:tq] + bias, s[tq:] + bias
    m0, m1 = _row_max(s0), _row_max(s1)
    return jnp.concatenate(_exp_rows(s0, m0) + _exp_rows(s1, m1), axis=1), jnp.concatenate([m0, m1], axis=0)


def _scores_head_blocked(qf, kf, bias, sink0, sink1):
    tk = kf.shape[0]
    slot0 = _pair_slot0()
    kb = kf.astype(bf16)
    zero = jnp.zeros_like(kb)
    k2 = jnp.concatenate([jnp.where(slot0, kb, zero), jnp.where(slot0, zero, kb)], axis=0)
    s = lax.dot_general(qf.astype(bf16), k2, (((1,), (1,)), ((), ())), preferred_element_type=f32)
    s0, s1 = s[:, :tk] + bias, s[:, tk:] + bias
    m0, m1 = jnp.maximum(_row_max(s0), sink0), jnp.maximum(_row_max(s1), sink1)
    return jnp.concatenate(_exp_rows(s0, m0) + _exp_rows(s1, m1), axis=1), jnp.concatenate([m0, m1], axis=0)


def _values(p, mb, vf):
    tq = p.shape[0]
    key_lane = lax.broadcasted_iota(jnp.int32, vf.shape, 1)
    sel0 = jnp.where(key_lane < HEAD_DIM, 1.0, 0.0).astype(bf16)
    sel1 = jnp.where(key_lane < HEAD_DIM, 0.0, 1.0).astype(bf16)
    vb = vf.astype(bf16)
    w = jnp.concatenate([jnp.concatenate([vb * sel0, sel0], axis=1),
                         jnp.concatenate([vb * sel1, sel1], axis=1)], axis=0)
    o = jnp.dot(p, w, preferred_element_type=f32)
    first = lax.broadcasted_iota(jnp.int32, (tq, LANES), 1) < HEAD_DIM
    return o[:, :LANES], jnp.where(first, mb[:tq], mb[tq:]), o[:, LANES:]


def _aligned(x, multiple):
    if isinstance(x, int):
        assert x % multiple == 0
        return x
    return pl.multiple_of(x, multiple)


def _window(q0, length, tq, tk, half_window):
    w0 = jnp.clip(q0 - half_window, 0, length - tk)
    variant = jnp.where(q0 == 0, 0, jnp.where(q0 == length - tq, 2, 1))
    return w0, variant


def _software_pipeline(groups, nb, stage1, stage2, between=(), values_first=False):
    n = len(groups)
    extra = list(between) + [None] * (n + 1 - len(between))
    for i in range(n + 1):
        if extra[i] is not None:
            extra[i]()

        def first_stage():
            if i < n:
                for u in range(nb):
                    stage1(groups[i], i % 2, u)

        def second_stage():
            if i > 0:
                for u in range(nb):
                    stage2(groups[i - 1], (i - 1) % 2, u)

        for part in ((second_stage, first_stage) if values_first else (first_stage, second_stage)):
            part()


def _mixer_a_kernel(q_ref, k_ref, v_ref, g_ref, bias_ref, o_ref,
                    k4_s, v4_s, q4_s, p_s, mb_s, acc_s, m_s, l_s, out_s, *, seq):
    st = pl.program_id(2)
    cls_len = seq // A_MID
    cls_rows = A_SUPER // A_MID

    @pl.when(st == 0)
    def _():
        def fill(c, carry):
            dst = pl.ds(_aligned(c * cls_rows, cls_rows), cls_rows)
            for r in range(A_MID):
                src = pl.ds(_aligned(c * A_SUPER, A_SUPER) + r, cls_rows, stride=A_MID)
                k4_s[r, dst, :] = k_ref[src, :]
                v4_s[r, dst, :] = v_ref[src, :]
            return carry
        lax.fori_loop(0, seq // A_SUPER, fill, 0)

    for r in range(A_MID):
        q4_s[r] = q_ref[pl.ds(r, cls_rows, stride=A_MID), :]

    nb = A_BLOCKS_PER_GROUP
    far_w0, far_variant = _window(st * A_TQ, seq // A_DILATIONS[2], A_TQ, A_TK, A_HALF_WINDOW)
    far_kbase = _aligned(far_w0 * A_MID, A_HALF_WINDOW)

    def geometry(kind, idx):
        if kind == 0:
            blk = idx
            w0, variant = _window(st * A_SUPER + blk * A_TQ, seq, A_TQ, A_TK, A_HALF_WINDOW)
            qrows = pl.ds(_aligned(blk * A_TQ, A_TQ), A_TQ)
            krows = pl.ds(_aligned(w0, A_HALF_WINDOW), A_TK)
            return (lambda: q_ref[qrows, :]), krows, k_ref, v_ref, variant, qrows
        g, u = divmod(idx, A_MID)
        if kind == 1:
            w0, variant = _window(st * cls_rows + g * A_TQ, cls_len, A_TQ, A_TK, A_HALF_WINDOW)
            qrows = pl.ds(_aligned(g * A_TQ, A_TQ), A_TQ)
            krows = pl.ds(_aligned(w0, A_HALF_WINDOW), A_TK)
            out_rows = pl.ds(_aligned(u * cls_rows + g * A_TQ, A_TQ), A_TQ)
            return (lambda: q4_s[u, qrows, :]), krows, k4_s.at[u], v4_s.at[u], variant, out_rows
        qrows = pl.ds(g, A_TQ, stride=A_MID)
        krows = pl.ds(far_kbase + g, A_TK, stride=A_MID)
        out_rows = pl.ds(u * cls_rows + g, A_TQ, stride=A_MID)
        return (lambda: q4_s[u, qrows, :]), krows, k4_s.at[u], v4_s.at[u], far_variant, out_rows

    def stage1(group, slot, u):
        kind, first_block = group
        load_q, krows, kk, _, variant, _ = geometry(kind, first_block + u)
        p_s[slot, u], mb_s[slot, u] = _scores_head_stacked(load_q(), kk[krows, :], bias_ref[variant])

    def stage2(group, slot, u):
        kind, first_block = group
        _, krows, _, vv, _, out_rows = geometry(kind, first_block + u)
        res = _values(p_s[slot, u], mb_s[slot, u], vv[krows, :])
        if kind != A_MERGE_KIND:
            side = A_SIDE_SLOT[kind]
            acc_s[side, out_rows, :], m_s[side, out_rows, :], l_s[side, out_rows, :] = res
            return
        g, cls = divmod(first_block + u, A_MID)
        tok = pl.ds(g * A_TQ * A_MID + cls, A_TQ, stride=A_MID)
        rows = {A_SIDE_SLOT[0]: tok, A_SIDE_SLOT[2]: out_rows}
        acc, m, l = res
        others = [(acc_s[side, r, :], m_s[side, r, :], l_s[side, r, :]) for side, r in rows.items()]
        m_all = m
        for _, m_o, _ in others:
            m_all = jnp.maximum(m_all, m_o)
        w = jnp.exp2(m - m_all)
        num, den = w * acc, w * l
        for acc_o, m_o, l_o in others:
            w = jnp.exp2(m_o - m_all)
            num, den = num + w * acc_o, den + w * l_o
        out_s[tok, :] = (num / den) * g_ref[tok, :]

    starts = range(0, A_SUPER // A_TQ, nb)
    groups = [(kind, blk) for blk in starts for kind in A_KIND_ORDER[:-1]] + [(A_MERGE_KIND, blk) for blk in starts]
    _software_pipeline(groups, nb, stage1, stage2, values_first=True)
    o_ref[...] = out_s[...].astype(o_ref.dtype)


def _mixer_a(qa, ka, va, ga, bias):
    b, s, _ = qa.shape
    nst = s // A_SUPER
    ntile = A_WIDTH // LANES
    tile = pl.BlockSpec((None, A_SUPER, LANES), lambda bi, t, st: (bi, st, t))
    full = pl.BlockSpec((None, s, LANES), lambda bi, t, st: (bi, 0, t))
    groups = pltpu.VMEM((len(A_SIDE_SLOT), A_SUPER, LANES), f32)
    return pl.pallas_call(
        functools.partial(_mixer_a_kernel, seq=s),
        grid=(b, ntile, nst),
        in_specs=[tile, full, full, tile,
                  pl.BlockSpec((3, A_TQ, A_TK), lambda bi, t, st: (0, 0, 0))],
        out_specs=tile,
        out_shape=jax.ShapeDtypeStruct((b, s, A_WIDTH), bf16),
        scratch_shapes=[pltpu.VMEM((A_MID, s // A_MID, LANES), f32),
                        pltpu.VMEM((A_MID, s // A_MID, LANES), f32),
                        pltpu.VMEM((A_MID, A_SUPER // A_MID, LANES), f32),
                        pltpu.VMEM((2, A_BLOCKS_PER_GROUP, A_TQ, 2 * A_TK), bf16),
                        pltpu.VMEM((2, A_BLOCKS_PER_GROUP, 2 * A_TQ, LANES), f32),
                        groups, groups, groups,
                        pltpu.VMEM((A_SUPER, LANES), f32)],
        compiler_params=pltpu.CompilerParams(
            dimension_semantics=("arbitrary", "arbitrary", "arbitrary"), vmem_limit_bytes=VMEM_LIMIT),
        name="mixer_a",
    )(qa, ka, va, ga, bias)


def _out_kernel(ya_ref, yb_ref, pa_ref, pb_ref, x_ref, wa_ref, wb_ref, wo_ref, lg_ref, lb_ref, o_ref):
    sub = x_ref.shape[0] // OUT_SUBTILES
    for i in range(OUT_SUBTILES):
        rows = pl.ds(i * sub, sub)
        bra = jnp.dot(ya_ref[rows, :], wa_ref[...], preferred_element_type=f32)
        brb = jnp.dot(yb_ref[rows, :], wb_ref[...], preferred_element_type=f32)
        merged = pa_ref[rows, :].astype(f32) * bra + pb_ref[rows, :].astype(f32) * brb
        out = jnp.dot(merged.astype(bf16), wo_ref[...], preferred_element_type=f32)
        z = DEEPNORM_ALPHA * x_ref[rows, :] + out
        mu = jnp.mean(z, axis=-1, keepdims=True)
        zc = z - mu
        var = jnp.mean(zc * zc, axis=-1, keepdims=True)
        o_ref[rows, :] = zc * lax.rsqrt(var + LN_EPS) * lg_ref[...] + lb_ref[...]


def _finish(ya, yb, pa, pb, x, wa, wb, wo, lg, lb, tm=1024):
    b, s, _ = x.shape
    tok = lambda n: pl.BlockSpec((None, tm, n), lambda bi, i: (bi, i, 0))
    const = lambda r, c: pl.BlockSpec((r, c), lambda bi, i: (0, 0))
    return pl.pallas_call(
        _out_kernel,
        grid=(b, s // tm),
        in_specs=[tok(A_WIDTH), tok(B_WIDTH), tok(D_MODEL), tok(D_MODEL), tok(D_MODEL),
                  const(A_WIDTH, D_MODEL), const(B_WIDTH, D_MODEL), const(D_MODEL, D_MODEL),
                  const(1, D_MODEL), const(1, D_MODEL)],
        out_specs=tok(D_MODEL),
        out_shape=jax.ShapeDtypeStruct((b, s, D_MODEL), f32),
        compiler_params=pltpu.CompilerParams(
            dimension_semantics=("parallel", "parallel"), vmem_limit_bytes=VMEM_LIMIT),
        name="out_proj",
    )(ya, yb, pa, pb, x, wa, wb, wo, lg, lb)


def _encoder_layer(x, w_p, b_gate, sink2, wa, wb, wo, lg, lb, bias_a, bias_b):
    s = x.shape[1]
    assert s % A_SUPER == 0 and s // max(A_DILATIONS) >= A_TK
    tables = _rope_tables(s)
    qa, ka, va, ga, pa, pb, yb = _project_and_mix_b(x, w_p, b_gate, sink2, tables, bias_b)
    ya = _mixer_a(qa, ka, va, ga, bias_a)
    return _finish(ya, yb, pa, pb, x, wa, wb, wo, lg, lb)


def kernel(x_prompt, x_sample, w_in, b_gate, sink_logit, w_branch_a, w_branch_b, w_out, ln_gain, ln_bias):
    assert w_in.shape[0] == DEPTH
    bias_a = jnp.asarray(_band_bias(A_TQ, A_TK, A_HALF_WINDOW))
    bias_b = jnp.asarray(_halo_bias(B_TQ, B_TK, B_HALF_WINDOW))
    y_prompt, y_sample = x_prompt, x_sample
    for l in range(DEPTH):
        w_p = _layout_w_in(w_in[l]).astype(bf16)
        params = (w_p, b_gate[l], sink_logit[l] * LOG2E,
                  w_branch_a[l].astype(bf16), _layout_w_branch_b(w_branch_b[l]).astype(bf16),
                  w_out[l].astype(bf16),
                  ln_gain[l][None, :], ln_bias[l][None, :], bias_a, bias_b)
        y_prompt = _encoder_layer(y_prompt, *params)
        y_sample = _encoder_layer(y_sample, *params)
    return (y_prompt, y_sample)
```

```python
import functools
import math

import numpy as np
import jax
import jax.numpy as jnp
from jax import lax
from jax.experimental import pallas as pl
from jax.experimental.pallas import tpu as pltpu

D_MODEL = 1024
HEAD_DIM = 64
HALF_DIM = HEAD_DIM // 2
A_HEADS = 8
A_WIDTH = A_HEADS * HEAD_DIM
A_DILATIONS = (1, 4, 16)
A_HALF_WINDOW = 64
B_HEADS = 8
B_KV_HEADS = 2
B_GROUP = B_HEADS // B_KV_HEADS
B_WIDTH = B_HEADS * HEAD_DIM
B_KV_WIDTH = B_KV_HEADS * HEAD_DIM
B_HALF_WINDOW = 128
ROPE_THETA = 10000.0
LN_EPS = 1e-5
NEG_INF = -1e30
DEPTH = 1
DEEPNORM_ALPHA = (2.0 * DEPTH) ** 0.25
IN_SPLIT_SIZES = (A_WIDTH, A_WIDTH, A_WIDTH, A_WIDTH, B_WIDTH, B_KV_WIDTH, B_KV_WIDTH, B_WIDTH, D_MODEL, D_MODEL)
IN_OFFSETS = tuple(int(v) for v in np.cumsum((0,) + IN_SPLIT_SIZES[:-1]))

LANES = 128
LOG2E = math.log2(math.e)
QK_SCALE = LOG2E / math.sqrt(HEAD_DIM)

A_TQ = 128
A_TK = A_TQ + 2 * A_HALF_WINDOW
A_SUPER = A_TQ * max(A_DILATIONS)
B_TQ = 128
B_TK = B_TQ + 2 * B_HALF_WINDOW
A_MID = A_DILATIONS[1]
assert A_DILATIONS == (1, A_MID, A_MID * A_MID)
OUT_SUBTILES = 4
A_BLOCKS_PER_GROUP = 1
A_KIND_ORDER = (0, 2, 1)
A_MERGE_KIND = A_KIND_ORDER[-1]
A_SIDE_SLOT = {0: 0, 2: 1}

VMEM_LIMIT = 56 * 1024 * 1024

f32 = jnp.float32
bf16 = jnp.bfloat16


B_PAIRS = B_HEADS // 2
assert B_KV_HEADS == 2


def _pair_tiles(w):
    d, t = w.shape[0], w.shape[1]
    return w.reshape(d, t, 2, 2, HALF_DIM).transpose(0, 1, 3, 2, 4).reshape(d, t * LANES)


def _layout_w_in(w_in):
    d = w_in.shape[0]
    seg = [w_in[:, o:o + n] for o, n in zip(IN_OFFSETS, IN_SPLIT_SIZES)]
    qa = _pair_tiles(seg[0].reshape(d, A_HEADS // 2, 2, HEAD_DIM))
    ka = _pair_tiles(seg[1].reshape(d, A_HEADS // 2, 2, HEAD_DIM))
    by_tile = lambda w: w.reshape(d, B_KV_HEADS, B_PAIRS, HEAD_DIM).transpose(0, 2, 1, 3)
    qb = _pair_tiles(by_tile(seg[4]))
    kb = _pair_tiles(seg[5].reshape(d, 1, B_KV_HEADS, HEAD_DIM))
    gb = by_tile(seg[7]).reshape(d, B_WIDTH)
    return jnp.concatenate([qa, ka, seg[2], seg[3], qb, kb, seg[6], gb, seg[8], seg[9]], axis=1)


def _layout_w_branch_b(w):
    return w.reshape(B_KV_HEADS, B_PAIRS, HEAD_DIM, -1).transpose(1, 0, 2, 3).reshape(w.shape)


P_QA, P_KA, P_VA, P_GA = 0, 512, 1024, 1536
P_QB, P_KB, P_GB = 2048, 2560, 2816
P_PA, P_PB = 3328, 4352
P_TOTAL = 5376
assert P_TOTAL == sum(IN_SPLIT_SIZES)


def _rope_tables(seq):
    inv_freq = ROPE_THETA ** (-jnp.arange(HALF_DIM, dtype=f32) / HALF_DIM)
    ang = jnp.arange(seq, dtype=f32)[:, None] * inv_freq[None, :]
    cos = jnp.tile(jnp.cos(ang), (1, 4))
    sin = jnp.tile(jnp.sin(ang), (1, 4))
    sign = jnp.where(jnp.arange(LANES) < 2 * HALF_DIM, -1.0, 1.0).astype(f32)
    sin = sin * sign[None, :]
    return cos, sin


def _band_bias(tq, tk, half_window):
    i = np.arange(tq)[:, None]
    j = np.arange(tk)[None, :]
    out = []
    for shift in (0, half_window, tk - tq):
        ok = np.abs(j - shift - i) <= half_window
        out.append(np.where(ok, 0.0, NEG_INF).astype(np.float32))
    return np.stack(out, 0)


def _halo_bias(tq, tk, half_window):
    i = np.arange(tq)[:, None]
    j = np.arange(tk)[None, :]
    band = np.abs(j - half_window - i) <= half_window
    out = [band & (j >= half_window), band, band & (j < tk - half_window)]
    return np.stack([np.where(ok, 0.0, NEG_INF).astype(np.float32) for ok in out], 0)


def _rope(h, c, s):
    tiles = []
    for t in range(h.shape[1] // LANES):
        ht = h[:, t * LANES:(t + 1) * LANES]
        tiles.append(ht * c + pltpu.roll(ht, 2 * HALF_DIM, 1) * s)
    return jnp.concatenate(tiles, axis=1)


def _proj_kernel(sink_ref, x_ref, xl_ref, xr_ref, w_ref, bg_ref, ck_ref, sk_ref,
                 ckl_ref, skl_ref, ckr_ref, skr_ref, bias_ref,
                 qa_o, ka_o, va_o, ga_o, pa_o, pb_o, yb_o,
                 qb_s, gb_s, kb_s, vb_s, p_s, mb_s):
    i, nt = pl.program_id(1), pl.num_programs(1)
    tm = x_ref.shape[0]
    xb = x_ref[...].astype(bf16)
    ck, sk = ck_ref[...], sk_ref[...]

    def proj(c0, n, lhs=xb):
        return jnp.dot(lhs, w_ref[:, c0:c0 + n], preferred_element_type=f32)

    x_ext = jnp.concatenate([xl_ref[...].astype(bf16), xb, xr_ref[...].astype(bf16)], axis=0)
    kvb = proj(P_KB, 2 * B_KV_WIDTH, x_ext)
    kb_s[...] = _rope(kvb[:, :B_KV_WIDTH], jnp.concatenate([ckl_ref[...], ck, ckr_ref[...]], axis=0),
                      jnp.concatenate([skl_ref[...], sk, skr_ref[...]], axis=0))
    vb_s[...] = kvb[:, B_KV_WIDTH:]
    qb_s[...] = _rope(proj(P_QB, B_WIDTH), ck, sk) * QK_SCALE
    gb_s[...] = jax.nn.silu(proj(P_GB, B_WIDTH))

    nb = tm // B_TQ
    first = lax.broadcasted_iota(jnp.int32, (B_TQ, LANES), 1) < HEAD_DIM

    def stage1(blk, slot, _):
        t, u = divmod(blk, nb)
        rows, cols = pl.ds(u * B_TQ, B_TQ), pl.ds(t * LANES, LANES)
        variant = jnp.where(i == 0, 0, 1) if u == 0 else jnp.where(i == nt - 1, 2, 1) if u == nb - 1 else 1
        sink0 = jnp.full((B_TQ, LANES), sink_ref[t], f32)
        sink1 = jnp.full((B_TQ, LANES), sink_ref[t + B_GROUP], f32)
        p_s[slot], mb_s[slot] = _scores_head_blocked(qb_s[rows, cols], kb_s[pl.ds(u * B_TQ, B_TK), :],
                                                    bias_ref[variant], sink0, sink1)

    def stage2(blk, slot, _):
        t, u = divmod(blk, nb)
        rows, cols = pl.ds(u * B_TQ, B_TQ), pl.ds(t * LANES, LANES)
        acc, m, l = _values(p_s[slot], mb_s[slot], vb_s[pl.ds(u * B_TQ, B_TK), :])
        l = l + jnp.exp2(jnp.where(first, sink_ref[t], sink_ref[t + B_GROUP]) - m)
        yb_o[rows, cols] = ((acc / l) * gb_s[rows, cols]).astype(yb_o.dtype)

    def seg_pa():
        pa_o[...] = jax.nn.sigmoid(proj(P_PA, D_MODEL) + bg_ref[0:1, :]).astype(pa_o.dtype)

    def seg_pb():
        pb_o[...] = jax.nn.sigmoid(proj(P_PB, D_MODEL) + bg_ref[1:2, :]).astype(pb_o.dtype)

    def seg_ga():
        ga_o[...] = jax.nn.silu(proj(P_GA, A_WIDTH))

    def seg_qa():
        qa_o[...] = _rope(proj(P_QA, A_WIDTH), ck, sk) * QK_SCALE

    def seg_ka():
        ka_o[...] = _rope(proj(P_KA, A_WIDTH), ck, sk)

    segments = (seg_pa, seg_pb, seg_ga, seg_qa, seg_ka)
    n_blocks = B_PAIRS * nb
    between = [None] * (n_blocks + 1)
    for j, seg in enumerate(segments):
        between[j * n_blocks // len(segments)] = seg
    _software_pipeline(list(range(n_blocks)), 1, stage1, stage2, between=between)
    va_o[...] = proj(P_VA, A_WIDTH)


def _project_and_mix_b(x, w_p, b_gate, sink2, tables, bias_b, tm=512):
    b, s, _ = x.shape
    halo = B_HALF_WINDOW
    per_tile, last = tm // halo, s // halo - 1
    tok = lambda n: pl.BlockSpec((None, tm, n), lambda bi, i, sk: (bi, i, 0))
    left = lambda i: jnp.maximum(i * per_tile - 1, 0)
    right = lambda i: jnp.minimum((i + 1) * per_tile, last)
    tab = pl.BlockSpec((tm, LANES), lambda bi, i, sk: (i, 0))
    tab_l = pl.BlockSpec((halo, LANES), lambda bi, i, sk: (left(i), 0))
    tab_r = pl.BlockSpec((halo, LANES), lambda bi, i, sk: (right(i), 0))
    ck, sk_ = tables
    widths = (A_WIDTH, A_WIDTH, A_WIDTH, A_WIDTH, D_MODEL, D_MODEL, B_WIDTH)
    dtypes = (f32, f32, f32, f32, bf16, bf16, bf16)
    return pl.pallas_call(
        _proj_kernel,
        grid_spec=pltpu.PrefetchScalarGridSpec(
            num_scalar_prefetch=1,
            grid=(b, s // tm),
            in_specs=[tok(D_MODEL),
                      pl.BlockSpec((None, halo, D_MODEL), lambda bi, i, sk: (bi, left(i), 0)),
                      pl.BlockSpec((None, halo, D_MODEL), lambda bi, i, sk: (bi, right(i), 0)),
                      pl.BlockSpec((D_MODEL, P_TOTAL), lambda bi, i, sk: (0, 0)),
                      pl.BlockSpec((2, D_MODEL), lambda bi, i, sk: (0, 0)),
                      tab, tab, tab_l, tab_l, tab_r, tab_r,
                      pl.BlockSpec((3, B_TQ, B_TK), lambda bi, i, sk: (0, 0, 0))],
            out_specs=[tok(n) for n in widths],
            scratch_shapes=[pltpu.VMEM((tm, B_WIDTH), f32), pltpu.VMEM((tm, B_WIDTH), f32),
                            pltpu.VMEM((tm + 2 * halo, B_KV_WIDTH), f32), pltpu.VMEM((tm + 2 * halo, B_KV_WIDTH), f32),
                            pltpu.VMEM((2, B_TQ, 2 * B_TK), bf16), pltpu.VMEM((2, 2 * B_TQ, LANES), f32)],
        ),
        out_shape=[jax.ShapeDtypeStruct((b, s, n), dt) for n, dt in zip(widths, dtypes)],
        compiler_params=pltpu.CompilerParams(
            dimension_semantics=("parallel", "parallel"), vmem_limit_bytes=VMEM_LIMIT),
        name="in_proj_mixer_b",
    )(sink2, x, x, x, w_p, b_gate, ck, sk_, ck, sk_, ck, sk_, bias_b)


def _pair_slot0():
    lane = lax.broadcasted_iota(jnp.int32, (1, LANES), 1)
    return (lane % HEAD_DIM) < HALF_DIM


def _exp_rows(s, mb):
    return [jnp.exp2(s[:, c * LANES:(c + 1) * LANES] - mb).astype(bf16) for c in range(s.shape[1] // LANES)]


def _row_max(s):
    return jnp.broadcast_to(jnp.max(s, axis=-1, keepdims=True), (s.shape[0], LANES))


def _scores_head_stacked(qf, kf, bias):
    tq = qf.shape[0]
    slot0 = _pair_slot0()
    qb = qf.astype(bf16)
    zero = jnp.zeros_like(qb)
    q2 = jnp.concatenate([jnp.where(slot0, qb, zero), jnp.where(slot0, zero, qb)], axis=0)
    s = lax.dot_general(q2, kf.astype(bf16), (((1,), (1,)), ((), ())), preferred_element_type=f32)
    s0, s1 = s[:tq] + bias, s[tq:] + bias
    m0, m1 = _row_max(s0), _row_max(s1)
    return jnp.concatenate(_exp_rows(s0, m0) + _exp_rows(s1, m1), axis=1), jnp.concatenate([m0, m1], axis=0)


def _scores_head_blocked(qf, kf, bias, sink0, sink1):
    tk = kf.shape[0]
    slot0 = _pair_slot0()
    kb = kf.astype(bf16)
    zero = jnp.zeros_like(kb)
    k2 = jnp.concatenate([jnp.where(slot0, kb, zero), jnp.where(slot0, zero, kb)], axis=0)
    s = lax.dot_general(qf.astype(bf16), k2, (((1,), (1,)), ((), ())), preferred_element_type=f32)
    s0, s1 = s[:, :tk] + bias, s[:, tk:] + bias
    m0, m1 = jnp.maximum(_row_max(s0), sink0), jnp.maximum(_row_max(s1), sink1)
    return jnp.concatenate(_exp_rows(s0, m0) + _exp_rows(s1, m1), axis=1), jnp.concatenate([m0, m1], axis=0)


def _values(p, mb, vf):
    tq = p.shape[0]
    key_lane = lax.broadcasted_iota(jnp.int32, vf.shape, 1)
    sel0 = jnp.where(key_lane < HEAD_DIM, 1.0, 0.0).astype(bf16)
    sel1 = jnp.where(key_lane < HEAD_DIM, 0.0, 1.0).astype(bf16)
    vb = vf.astype(bf16)
    w = jnp.concatenate([jnp.concatenate([vb * sel0, sel0], axis=1),
                         jnp.concatenate([vb * sel1, sel1], axis=1)], axis=0)
    o = jnp.dot(p, w, preferred_element_type=f32)
    first = lax.broadcasted_iota(jnp.int32, (tq, LANES), 1) < HEAD_DIM
    return o[:, :LANES], jnp.where(first, mb[:tq], mb[tq:]), o[:, LANES:]


def _aligned(x, multiple):
    if isinstance(x, int):
        assert x % multiple == 0
        return x
    return pl.multiple_of(x, multiple)


def _window(q0, length, tq, tk, half_window):
    w0 = jnp.clip(q0 - half_window, 0, length - tk)
    variant = jnp.where(q0 == 0, 0, jnp.where(q0 == length - tq, 2, 1))
    return w0, variant


def _software_pipeline(groups, nb, stage1, stage2, between=(), values_first=False):
    n = len(groups)
    extra = list(between) + [None] * (n + 1 - len(between))
    for i in range(n + 1):
        if extra[i] is not None:
            extra[i]()

        def first_stage():
            if i < n:
                for u in range(nb):
                    stage1(groups[i], i % 2, u)

        def second_stage():
            if i > 0:
                for u in range(nb):
                    stage2(groups[i - 1], (i - 1) % 2, u)

        for part in ((second_stage, first_stage) if values_first else (first_stage, second_stage)):
            part()


def _mixer_a_kernel(q_ref, k_ref, v_ref, g_ref, bias_ref, o_ref,
                    k4_s, v4_s, q4_s, p_s, mb_s, acc_s, m_s, l_s, out_s, *, seq):
    st = pl.program_id(2)
    cls_len = seq // A_MID
    cls_rows = A_SUPER // A_MID

    @pl.when(st == 0)
    def _():
        def fill(c, carry):
            dst = pl.ds(_aligned(c * cls_rows, cls_rows), cls_rows)
            for r in range(A_MID):
                src = pl.ds(_aligned(c * A_SUPER, A_SUPER) + r, cls_rows, stride=A_MID)
                k4_s[r, dst, :] = k_ref[src, :]
                v4_s[r, dst, :] = v_ref[src, :]
            return carry
        lax.fori_loop(0, seq // A_SUPER, fill, 0)

    for r in range(A_MID):
        q4_s[r] = q_ref[pl.ds(r, cls_rows, stride=A_MID), :]

    nb = A_BLOCKS_PER_GROUP
    far_w0, far_variant = _window(st * A_TQ, seq // A_DILATIONS[2], A_TQ, A_TK, A_HALF_WINDOW)
    far_kbase = _aligned(far_w0 * A_MID, A_HALF_WINDOW)

    def geometry(kind, idx):
        if kind == 0:
            blk = idx
            w0, variant = _window(st * A_SUPER + blk * A_TQ, seq, A_TQ, A_TK, A_HALF_WINDOW)
            qrows = pl.ds(_aligned(blk * A_TQ, A_TQ), A_TQ)
            krows = pl.ds(_aligned(w0, A_HALF_WINDOW), A_TK)
            return (lambda: q_ref[qrows, :]), krows, k_ref, v_ref, variant, qrows
        g, u = divmod(idx, A_MID)
        if kind == 1:
            w0, variant = _window(st * cls_rows + g * A_TQ, cls_len, A_TQ, A_TK, A_HALF_WINDOW)
            qrows = pl.ds(_aligned(g * A_TQ, A_TQ), A_TQ)
            krows = pl.ds(_aligned(w0, A_HALF_WINDOW), A_TK)
            out_rows = pl.ds(_aligned(u * cls_rows + g * A_TQ, A_TQ), A_TQ)
            return (lambda: q4_s[u, qrows, :]), krows, k4_s.at[u], v4_s.at[u], variant, out_rows
        qrows = pl.ds(g, A_TQ, stride=A_MID)
        krows = pl.ds(far_kbase + g, A_TK, stride=A_MID)
        out_rows = pl.ds(u * cls_rows + g, A_TQ, stride=A_MID)
        return (lambda: q4_s[u, qrows, :]), krows, k4_s.at[u], v4_s.at[u], far_variant, out_rows

    def stage1(group, slot, u):
        kind, first_block = group
        load_q, krows, kk, _, variant, _ = geometry(kind, first_block + u)
        p_s[slot, u], mb_s[slot, u] = _scores_head_stacked(load_q(), kk[krows, :], bias_ref[variant])

    def stage2(group, slot, u):
        kind, first_block = group
        _, krows, _, vv, _, out_rows = geometry(kind, first_block + u)
        res = _values(p_s[slot, u], mb_s[slot, u], vv[krows, :])
        if kind != A_MERGE_KIND:
            side = A_SIDE_SLOT[kind]
            acc_s[side, out_rows, :], m_s[side, out_rows, :], l_s[side, out_rows, :] = res
            return
        g, cls = divmod(first_block + u, A_MID)
        tok = pl.ds(g * A_TQ * A_MID + cls, A_TQ, stride=A_MID)
        rows = {A_SIDE_SLOT[0]: tok, A_SIDE_SLOT[2]: out_rows}
        acc, m, l = res
        others = [(acc_s[side, r, :], m_s[side, r, :], l_s[side, r, :]) for side, r in rows.items()]
        m_all = m
        for _, m_o, _ in others:
            m_all = jnp.maximum(m_all, m_o)
        w = jnp.exp2(m - m_all)
        num, den = w * acc, w * l
        for acc_o, m_o, l_o in others:
            w = jnp.exp2(m_o - m_all)
            num, den = num + w * acc_o, den + w * l_o
        out_s[tok, :] = (num / den) * g_ref[tok, :]

    starts = range(0, A_SUPER // A_TQ, nb)
    groups = [(kind, blk) for blk in starts for kind in A_KIND_ORDER[:-1]] + [(A_MERGE_KIND, blk) for blk in starts]
    _software_pipeline(groups, nb, stage1, stage2, values_first=True)
    o_ref[...] = out_s[...].astype(o_ref.dtype)


def _mixer_a(qa, ka, va, ga, bias):
    b, s, _ = qa.shape
    nst = s // A_SUPER
    ntile = A_WIDTH // LANES
    tile = pl.BlockSpec((None, A_SUPER, LANES), lambda bi, t, st: (bi, st, t))
    full = pl.BlockSpec((None, s, LANES), lambda bi, t, st: (bi, 0, t))
    groups = pltpu.VMEM((len(A_SIDE_SLOT), A_SUPER, LANES), f32)
    return pl.pallas_call(
        functools.partial(_mixer_a_kernel, seq=s),
        grid=(b, ntile, nst),
        in_specs=[tile, full, full, tile,
                  pl.BlockSpec((3, A_TQ, A_TK), lambda bi, t, st: (0, 0, 0))],
        out_specs=tile,
        out_shape=jax.ShapeDtypeStruct((b, s, A_WIDTH), bf16),
        scratch_shapes=[pltpu.VMEM((A_MID, s // A_MID, LANES), f32),
                        pltpu.VMEM((A_MID, s // A_MID, LANES), f32),
                        pltpu.VMEM((A_MID, A_SUPER // A_MID, LANES), f32),
                        pltpu.VMEM((2, A_BLOCKS_PER_GROUP, A_TQ, 2 * A_TK), bf16),
                        pltpu.VMEM((2, A_BLOCKS_PER_GROUP, 2 * A_TQ, LANES), f32),
                        groups, groups, groups,
                        pltpu.VMEM((A_SUPER, LANES), f32)],
        compiler_params=pltpu.CompilerParams(
            dimension_semantics=("arbitrary", "arbitrary", "arbitrary"), vmem_limit_bytes=VMEM_LIMIT),
        name="mixer_a",
    )(qa, ka, va, ga, bias)


def _out_kernel(ya_ref, yb_ref, pa_ref, pb_ref, x_ref, wa_ref, wb_ref, wo_ref, lg_ref, lb_ref, o_ref):
    sub = x_ref.shape[0] // OUT_SUBTILES
    for i in range(OUT_SUBTILES):
        rows = pl.ds(i * sub, sub)
        bra = jnp.dot(ya_ref[rows, :], wa_ref[...], preferred_element_type=f32)
        brb = jnp.dot(yb_ref[rows, :], wb_ref[...], preferred_element_type=f32)
        merged = pa_ref[rows, :].astype(f32) * bra + pb_ref[rows, :].astype(f32) * brb
        out = jnp.dot(merged.astype(bf16), wo_ref[...], preferred_element_type=f32)
        z = DEEPNORM_ALPHA * x_ref[rows, :] + out
        mu = jnp.mean(z, axis=-1, keepdims=True)
        zc = z - mu
        var = jnp.mean(zc * zc, axis=-1, keepdims=True)
        o_ref[rows, :] = zc * lax.rsqrt(var + LN_EPS) * lg_ref[...] + lb_ref[...]


def _finish(ya, yb, pa, pb, x, wa, wb, wo, lg, lb, tm=1024):
    b, s, _ = x.shape
    tok = lambda n: pl.BlockSpec((None, tm, n), lambda bi, i: (bi, i, 0))
    const = lambda r, c: pl.BlockSpec((r, c), lambda bi, i: (0, 0))
    return pl.pallas_call(
        _out_kernel,
        grid=(b, s // tm),
        in_specs=[tok(A_WIDTH), tok(B_WIDTH), tok(D_MODEL), tok(D_MODEL), tok(D_MODEL),
                  const(A_WIDTH, D_MODEL), const(B_WIDTH, D_MODEL), const(D_MODEL, D_MODEL),
                  const(1, D_MODEL), const(1, D_MODEL)],
        out_specs=tok(D_MODEL),
        out_shape=jax.ShapeDtypeStruct((b, s, D_MODEL), f32),
        compiler_params=pltpu.CompilerParams(
            dimension_semantics=("parallel", "parallel"), vmem_limit_bytes=VMEM_LIMIT),
        name="out_proj",
    )(ya, yb, pa, pb, x, wa, wb, wo, lg, lb)


def _encoder_layer(x, tables, w_p, b_gate, sink2, wa, wb, wo, lg, lb, bias_a, bias_b):
    s = x.shape[1]
    assert s % A_SUPER == 0 and s // max(A_DILATIONS) >= A_TK
    qa, ka, va, ga, pa, pb, yb = _project_and_mix_b(x, w_p, b_gate, sink2, tables, bias_b)
    ya = _mixer_a(qa, ka, va, ga, bias_a)
    return _finish(ya, yb, pa, pb, x, wa, wb, wo, lg, lb)


def kernel(x_prompt, x_sample, w_in, b_gate, sink_logit, w_branch_a, w_branch_b, w_out, ln_gain, ln_bias):
    assert w_in.shape[0] == DEPTH
    bias_a = jnp.asarray(_band_bias(A_TQ, A_TK, A_HALF_WINDOW))
    bias_b = jnp.asarray(_halo_bias(B_TQ, B_TK, B_HALF_WINDOW))
    y_prompt, y_sample = x_prompt, x_sample
    tables = _rope_tables(max(x_prompt.shape[1], x_sample.shape[1]))
    for l in range(DEPTH):
        w_p = _layout_w_in(w_in[l]).astype(bf16)
        params = (w_p, b_gate[l], sink_logit[l] * LOG2E,
                  w_branch_a[l].astype(bf16), _layout_w_branch_b(w_branch_b[l]).astype(bf16),
                  w_out[l].astype(bf16),
                  ln_gain[l][None, :], ln_bias[l][None, :], bias_a, bias_b)
        y_prompt = _encoder_layer(y_prompt, tables, *params)
        y_sample = _encoder_layer(y_sample, tables, *params)
    return (y_prompt, y_sample)
```
